```python
import math
import jax
import jax.numpy as jnp
from jax import lax
import numpy as np

D_MODEL = 1024
BATCH = 4
SEQ = 4096
DEPTH = 4
DEC_BATCH = 32
DEC_SEQ = 1
PAST_LEN = 8192
PAGE_SIZE = 128

MIXER_KINDS = ('diff', 'moba', 'rglru')
N_MIXERS = len(MIXER_KINDS)
HEAD_DIM = 64
DIFF_HEADS = D_MODEL // (2 * HEAD_DIM)
MOBA_HEADS = D_MODEL // HEAD_DIM
MOBA_BLOCK = 256
MOBA_TOPK = 3
MOBA_Q_CHUNK = 16
ATTN_Q_BLOCK = 128
RG_WIDTH = 3 * D_MODEL // 2
RG_BLOCKS = 12
RG_BW = RG_WIDTH // RG_BLOCKS
RG_CONV = 4
RG_C = 8.0
FFN_DIM = 128 * (-(-8 * D_MODEL // (3 * 128)))
FFN_CONV = 3
ROPE_THETA = 10000.0
NORM_EPS = 1e-5
DEEPNORM_ALPHA = (2.0 * DEPTH) ** 0.25
DEEPNORM_BETA = (8.0 * DEPTH) ** -0.25

kernel_name = 'hybrid_diffattn_moba_rglru_convffn_step'

F32 = jnp.float32


def layer_norm(x, g, b):
    xf = x.astype(F32)
    mu = jnp.mean(xf, -1, keepdims=True)
    xc = xf - mu
    var = jnp.mean(xc * xc, -1, keepdims=True)
    return (xc * lax.rsqrt(var + NORM_EPS) * g.astype(F32) + b.astype(F32)).astype(x.dtype)


def rms_norm(x, g):
    xf = x.astype(F32)
    return xf * lax.rsqrt(jnp.mean(xf * xf, -1, keepdims=True) + NORM_EPS) * g.astype(F32)


def rope(x, pos):
    half = x.shape[-1] // 2
    inv_freq = ROPE_THETA ** (-jnp.arange(half, dtype=F32) / half)
    ang = pos.astype(F32)[:, None] * inv_freq[None, :]
    cos = jnp.cos(ang)[None, :, None, :]
    sin = jnp.sin(ang)[None, :, None, :]
    x1 = x[..., :half].astype(F32)
    x2 = x[..., half:].astype(F32)
    return jnp.concatenate([x1 * cos - x2 * sin, x2 * cos + x1 * sin], -1).astype(x.dtype)


def causal_dwconv(u, buf, w, b):
    width = w.shape[0]
    t = u.shape[1]
    ext = jnp.concatenate([buf.astype(u.dtype), u], axis=1)
    y = b.astype(u.dtype) + ext[:, 0:t] * w[0].astype(u.dtype)
    for j in range(1, width):
        y = y + ext[:, j:j + t] * w[j].astype(u.dtype)
    return y, ext[:, t:]


def gather_pages(cache, page_table):
    pages = cache[page_table]
    return pages.reshape(pages.shape[0], pages.shape[1] * pages.shape[2], *cache.shape[2:])


def query_block_map(fn, q, qpos, block):
    b, t = q.shape[:2]
    nb = t // block
    qb = jnp.swapaxes(q.reshape(b, nb, block, *q.shape[2:]), 0, 1)
    pb = qpos.reshape(nb, block)
    ob = lax.map(lambda a: fn(a[0], a[1]), (qb, pb))
    return jnp.swapaxes(ob, 0, 1).reshape(b, t, *ob.shape[3:])


def diff_qkv(x, w_qkv, pos):
    b, t, _ = x.shape
    q, k, v = jnp.split(x @ w_qkv, 3, axis=-1)
    q = rope(q.reshape(b, t, 2 * DIFF_HEADS, HEAD_DIM), pos)
    k = rope(k.reshape(b, t, 2 * DIFF_HEADS, HEAD_DIM), pos)
    v = v.reshape(b, t, DIFF_HEADS, 2 * HEAD_DIM)
    return q, k, v


def diff_core(q, k, v, qpos, kpos, lam):
    s = jnp.einsum('bqhd,bkhd->bhqk', q, k, preferred_element_type=F32) * HEAD_DIM ** -0.5
    s = jnp.where(kpos[None, :] <= qpos[:, None], s, -jnp.inf)
    p = jax.nn.softmax(s, axis=-1)
    b, _, nq, nk = p.shape
    p = p.reshape(b, DIFF_HEADS, 2, nq, nk)
    p = p[:, :, 0] - lam * p[:, :, 1]
    return jnp.einsum('bhqk,bkhe->bqhe', p.astype(v.dtype), v)


def diff_out(o, subln, lam_init, w_o):
    b, t = o.shape[:2]
    on = (rms_norm(o, subln) * (1.0 - lam_init)).astype(o.dtype)
    return on.reshape(b, t, D_MODEL) @ w_o


def diff_layer(x_p, x_s, w_qkv, lam_p, subln, w_o, cache_k, cache_v, page_table, pos_p, pos_s, layer_idx):
    lam_init = 0.8 - 0.6 * math.exp(-0.3 * layer_idx)
    lp = lam_p.astype(F32)
    lam = jnp.exp(jnp.sum(lp[0] * lp[1])) - jnp.exp(jnp.sum(lp[2] * lp[3])) + lam_init
    qp, kp, vp = diff_qkv(x_p, w_qkv, pos_p)
    op = query_block_map(lambda qb, pb: diff_core(qb, kp, vp, pb, pos_p, lam), qp, pos_p, ATTN_Q_BLOCK)
    qs, ks, vs = diff_qkv(x_s, w_qkv, pos_s)
    k_all = jnp.concatenate([gather_pages(cache_k, page_table).astype(ks.dtype), ks], axis=1)
    v_all = jnp.concatenate([gather_pages(cache_v, page_table).astype(vs.dtype), vs], axis=1)
    kpos_all = jnp.arange(k_all.shape[1], dtype=jnp.int32)
    os_ = diff_core(qs, k_all, v_all, pos_s, kpos_all, lam)
    return (diff_out(op, subln, lam_init, w_o), diff_out(os_, subln, lam_init, w_o), (kp, vp), (ks, vs))


def moba_qkv(x, w_qkv, pos):
    b, t, _ = x.shape
    q, k, v = jnp.split(x @ w_qkv, 3, axis=-1)
    shape = (b, t, MOBA_HEADS, HEAD_DIM)
    return rope(q.reshape(shape), pos), rope(k.reshape(shape), pos), v.reshape(shape)


def moba_blocks(k, v):
    b, t = k.shape[:2]
    nb = -(-t // MOBA_BLOCK)
    pad = ((0, 0), (0, nb * MOBA_BLOCK - t), (0, 0), (0, 0))
    kb = jnp.pad(k, pad).reshape(b, nb, MOBA_BLOCK, MOBA_HEADS, HEAD_DIM)
    vb = jnp.pad(v, pad).reshape(b, nb, MOBA_BLOCK, MOBA_HEADS, HEAD_DIM)
    means = jnp.mean(kb.astype(F32), axis=2)
    return kb.transpose(0, 3, 1, 2, 4), vb.transpose(0, 3, 1, 2, 4), means


def gather_blocks(blocks, idx):
    return jax.vmap(jax.vmap(lambda t, ix: t[ix]))(blocks, idx)


def moba_core(q, qpos, kbh, vbh, means):
    b, nq = q.shape[:2]
    nb = means.shape[1]
    gate = jnp.einsum('bqhd,bnhd->bhqn', q.astype(F32), means)
    own = qpos // MOBA_BLOCK
    fully_past = jnp.arange(nb, dtype=jnp.int32)[None, :] < own[:, None]
    gate = jnp.where(fully_past, gate, -jnp.inf)
    top_vals, top_idx = lax.top_k(gate, min(MOBA_TOPK, nb))
    own_idx = jnp.broadcast_to(own[None, None, :, None], (b, MOBA_HEADS, nq, 1)).astype(top_idx.dtype)
    idx = jnp.concatenate([top_idx, own_idx], axis=-1)
    slot_ok = jnp.concatenate([jnp.isfinite(top_vals), jnp.ones((b, MOBA_HEADS, nq, 1), bool)], axis=-1)
    kg = gather_blocks(kbh, idx)
    vg = gather_blocks(vbh, idx)
    kpos = idx[..., None] * MOBA_BLOCK + jnp.arange(MOBA_BLOCK, dtype=jnp.int32)
    mask = slot_ok[..., None] & (kpos <= qpos[None, None, :, None, None])
    s = jnp.einsum('bqhd,bhqsjd->bhqsj', q, kg, preferred_element_type=F32) * HEAD_DIM ** -0.5
    s = jnp.where(mask, s, -jnp.inf)
    p = jax.nn.softmax(s.reshape(*s.shape[:3], -1), axis=-1).reshape(s.shape)
    return jnp.einsum('bhqsj,bhqsjd->bqhd', p.astype(vg.dtype), vg)


def moba_layer(x_p, x_s, w_qkv, w_o, cache_k, cache_v, page_table, pos_p, pos_s):
    b, t, _ = x_p.shape
    qp, kp, vp = moba_qkv(x_p, w_qkv, pos_p)
    kbh, vbh, means = moba_blocks(kp, vp)
    op = query_block_map(lambda qb, pb: moba_core(qb, pb, kbh, vbh, means), qp, pos_p, MOBA_Q_CHUNK)
    qs, ks, vs = moba_qkv(x_s, w_qkv, pos_s)
    k_all = jnp.concatenate([gather_pages(cache_k, page_table).astype(ks.dtype), ks], axis=1)
    v_all = jnp.concatenate([gather_pages(cache_v, page_table).astype(vs.dtype), vs], axis=1)
    kbh_s, vbh_s, means_s = moba_blocks(k_all, v_all)
    os_ = moba_core(qs, pos_s, kbh_s, vbh_s, means_s)
    y_p = op.reshape(b, t, D_MODEL) @ w_o
    y_s = os_.reshape(x_s.shape[0], x_s.shape[1], D_MODEL) @ w_o
    return y_p, y_s, (kp, vp), (ks, vs)


def rglru_scan(u, h0, w_a, b_a, w_i, b_i, lam):
    b, t, _ = u.shape
    uf = u.astype(F32)
    ub = uf.reshape(b, t, RG_BLOCKS, RG_BW)
    r = jax.nn.sigmoid(jnp.einsum('btnj,njk->btnk', ub, w_a.astype(F32)).reshape(b, t, RG_WIDTH) + b_a.astype(F32))
    i = jax.nn.sigmoid(jnp.einsum('btnj,njk->btnk', ub, w_i.astype(F32)).reshape(b, t, RG_WIDTH) + b_i.astype(F32))
    log_a = -RG_C * r * jax.nn.softplus(-lam.astype(F32))
    a = jnp.exp(log_a)
    bx = jnp.sqrt(-jnp.expm1(2.0 * log_a)) * (i * uf)
    bx = bx.at[:, 0].add(a[:, 0] * h0.astype(F32))

    def combine(lhs, rhs):
        a1, b1 = lhs
        a2, b2 = rhs
        return a1 * a2, a2 * b1 + b2

    _, h = lax.associative_scan(combine, (a, bx), axis=1)
    return h, h[:, -1]


def rglru_branch(x, h0, buf, w_in, conv_w, conv_b, w_a, b_a, w_i, b_i, lam, w_o):
    gate, u = jnp.split(x @ w_in, 2, axis=-1)
    uc, new_buf = causal_dwconv(u, buf, conv_w, conv_b)
    h, h_last = rglru_scan(uc, h0, w_a, b_a, w_i, b_i, lam)
    y = (jax.nn.gelu(gate.astype(F32)) * h).astype(x.dtype) @ w_o
    return y, h_last, new_buf


def rglru_layer(x_p, x_s, w_in, conv_w, conv_b, w_a, b_a, w_i, b_i, lam, w_o, h_state, conv_state):
    b = x_p.shape[0]
    h0_p = jnp.zeros((b, RG_WIDTH), F32)
    buf_p = jnp.zeros((b, RG_CONV - 1, RG_WIDTH), x_p.dtype)
    y_p, h_p, cb_p = rglru_branch(x_p, h0_p, buf_p, w_in, conv_w, conv_b, w_a, b_a, w_i, b_i, lam, w_o)
    y_s, h_s, cb_s = rglru_branch(x_s, h_state, conv_state, w_in, conv_w, conv_b, w_a, b_a, w_i, b_i, lam, w_o)
    return y_p, y_s, (h_p.astype(x_p.dtype), cb_p), (h_s.astype(h_state.dtype), cb_s)


def conv_ffn(x, buf, w_up, conv_w, conv_b, w_down):
    a, g = jnp.split(x @ w_up, 2, axis=-1)
    ac, new_buf = causal_dwconv(a, buf, conv_w, conv_b)
    h = (jax.nn.gelu(ac.astype(F32)) * g.astype(F32)).astype(x.dtype)
    return h @ w_down, new_buf


def setup_inputs(seed: int = 0) -> dict:
    key = jax.random.key(seed)
    keys = list(jax.random.split(key, 48))

    def nrm(shape, scale):
        return scale * jax.random.normal(keys.pop(), shape, F32)

    n_pages = PAST_LEN // PAGE_SIZE
    n_used = DEC_BATCH * n_pages
    n_pool = n_used + max(1, n_used // 4)
    page_table = jax.random.permutation(keys.pop(), n_pool)[:n_used].reshape(DEC_BATCH, n_pages).astype(jnp.int32)

    u = jax.random.uniform(keys.pop(), (RG_WIDTH,), F32, 0.9, 0.999)
    s = u ** (1.0 / RG_C)
    l2_lambda = jnp.log(s) - jnp.log1p(-s)

    dk = D_MODEL ** -0.5
    return {
        'x_prompt': nrm((BATCH, SEQ, D_MODEL), 1.0),
        'x_sample': nrm((DEC_BATCH, DEC_SEQ, D_MODEL), 1.0),
        'cache_k_l0': nrm((n_pool, PAGE_SIZE, 2 * DIFF_HEADS, HEAD_DIM), 1.0),
        'cache_v_l0': nrm((n_pool, PAGE_SIZE, DIFF_HEADS, 2 * HEAD_DIM), 1.0),
        'cache_k_l1': nrm((n_pool, PAGE_SIZE, MOBA_HEADS, HEAD_DIM), 1.0),
        'cache_v_l1': nrm((n_pool, PAGE_SIZE, MOBA_HEADS, HEAD_DIM), 1.0),
        'cache_k_l3': nrm((n_pool, PAGE_SIZE, 2 * DIFF_HEADS, HEAD_DIM), 1.0),
        'cache_v_l3': nrm((n_pool, PAGE_SIZE, DIFF_HEADS, 2 * HEAD_DIM), 1.0),
        'state_rg_h_l2': nrm((DEC_BATCH, RG_WIDTH), 0.5),
        'state_rg_conv_l2': nrm((DEC_BATCH, RG_CONV - 1, RG_WIDTH), 1.0),
        'state_ffn_conv': nrm((DEPTH, DEC_BATCH, FFN_CONV - 1, FFN_DIM), 1.0),
        'page_table': page_table,
        'l0_w_qkv': nrm((D_MODEL, 3 * D_MODEL), dk),
        'l0_lambda': nrm((4, HEAD_DIM), 0.1),
        'l0_subln': 1.0 + nrm((2 * HEAD_DIM,), 0.01),
        'l0_w_o': nrm((D_MODEL, D_MODEL), dk * DEEPNORM_BETA),
        'l1_w_qkv': nrm((D_MODEL, 3 * D_MODEL), dk),
        'l1_w_o': nrm((D_MODEL, D_MODEL), dk * DEEPNORM_BETA),
        'l2_w_in': nrm((D_MODEL, 2 * RG_WIDTH), dk),
        'l2_conv_w': nrm((RG_CONV, RG_WIDTH), RG_CONV ** -0.5),
        'l2_conv_b': nrm((RG_WIDTH,), 0.01),
        'l2_w_a': nrm((RG_BLOCKS, RG_BW, RG_BW), RG_BW ** -0.5),
        'l2_b_a': nrm((RG_WIDTH,), 0.01),
        'l2_w_i': nrm((RG_BLOCKS, RG_BW, RG_BW), RG_BW ** -0.5),
        'l2_b_i': nrm((RG_WIDTH,), 0.01),
        'l2_lambda': l2_lambda,
        'l2_w_o': nrm((RG_WIDTH, D_MODEL), RG_WIDTH ** -0.5 * DEEPNORM_BETA),
        'l3_w_qkv': nrm((D_MODEL, 3 * D_MODEL), dk),
        'l3_lambda': nrm((4, HEAD_DIM), 0.1),
        'l3_subln': 1.0 + nrm((2 * HEAD_DIM,), 0.01),
        'l3_w_o': nrm((D_MODEL, D_MODEL), dk * DEEPNORM_BETA),
        'ffn_w_up': nrm((DEPTH, D_MODEL, 2 * FFN_DIM), dk),
        'ffn_conv_w': nrm((DEPTH, FFN_CONV, FFN_DIM), FFN_CONV ** -0.5),
        'ffn_conv_b': nrm((DEPTH, FFN_DIM), 0.01),
        'ffn_w_down': nrm((DEPTH, FFN_DIM, D_MODEL), FFN_DIM ** -0.5 * DEEPNORM_BETA),
        'ln_mix_g': 1.0 + nrm((DEPTH, D_MODEL), 0.01),
        'ln_mix_b': nrm((DEPTH, D_MODEL), 0.01),
        'ln_ffn_g': 1.0 + nrm((DEPTH, D_MODEL), 0.01),
        'ln_ffn_b': nrm((DEPTH, D_MODEL), 0.01),
    }


def reference(x_prompt, x_sample, cache_k_l0, cache_v_l0, cache_k_l1, cache_v_l1, cache_k_l3, cache_v_l3,
              state_rg_h_l2, state_rg_conv_l2, state_ffn_conv, page_table,
              l0_w_qkv, l0_lambda, l0_subln, l0_w_o, l1_w_qkv, l1_w_o,
              l2_w_in, l2_conv_w, l2_conv_b, l2_w_a, l2_b_a, l2_w_i, l2_b_i, l2_lambda, l2_w_o,
              l3_w_qkv, l3_lambda, l3_subln, l3_w_o,
              ffn_w_up, ffn_conv_w, ffn_conv_b, ffn_w_down, ln_mix_g, ln_mix_b, ln_ffn_g, ln_ffn_b):
    pos_p = jnp.arange(x_prompt.shape[1], dtype=jnp.int32)
    past_len = page_table.shape[1] * PAGE_SIZE
    pos_s = past_len + jnp.arange(x_sample.shape[1], dtype=jnp.int32)
    layer_params = (
        (l0_w_qkv, l0_lambda, l0_subln, l0_w_o, cache_k_l0, cache_v_l0),
        (l1_w_qkv, l1_w_o, cache_k_l1, cache_v_l1),
        (l2_w_in, l2_conv_w, l2_conv_b, l2_w_a, l2_b_a, l2_w_i, l2_b_i, l2_lambda, l2_w_o,
         state_rg_h_l2, state_rg_conv_l2),
        (l3_w_qkv, l3_lambda, l3_subln, l3_w_o, cache_k_l3, cache_v_l3),
    )
    xp, xs = x_prompt, x_sample
    mix_p, mix_s, ffn_p, ffn_s = [], [], [], []
    for i in range(DEPTH):
        kind = MIXER_KINDS[i % N_MIXERS]
        if kind == 'diff':
            yp, ys, sp, ss = diff_layer(xp, xs, *layer_params[i], page_table, pos_p, pos_s, i)
        elif kind == 'moba':
            yp, ys, sp, ss = moba_layer(xp, xs, *layer_params[i], page_table, pos_p, pos_s)
        else:
            yp, ys, sp, ss = rglru_layer(xp, xs, *layer_params[i])
        xp = layer_norm(DEEPNORM_ALPHA * xp + yp, ln_mix_g[i], ln_mix_b[i])
        xs = layer_norm(DEEPNORM_ALPHA * xs + ys, ln_mix_g[i], ln_mix_b[i])
        zero_buf = jnp.zeros((xp.shape[0], FFN_CONV - 1, FFN_DIM), xp.dtype)
        fp, bp = conv_ffn(xp, zero_buf, ffn_w_up[i], ffn_conv_w[i], ffn_conv_b[i], ffn_w_down[i])
        fs, bs = conv_ffn(xs, state_ffn_conv[i], ffn_w_up[i], ffn_conv_w[i], ffn_conv_b[i], ffn_w_down[i])
        xp = layer_norm(DEEPNORM_ALPHA * xp + fp, ln_ffn_g[i], ln_ffn_b[i])
        xs = layer_norm(DEEPNORM_ALPHA * xs + fs, ln_ffn_g[i], ln_ffn_b[i])
        mix_p.append(sp)
        mix_s.append(ss)
        ffn_p.append(bp)
        ffn_s.append(bs)
    (k_l0_p, v_l0_p), (k_l1_p, v_l1_p), (rg_h_p, rg_conv_p), (k_l3_p, v_l3_p) = mix_p
    (k_l0_s, v_l0_s), (k_l1_s, v_l1_s), (rg_h_s, rg_conv_s), (k_l3_s, v_l3_s) = mix_s
    ffn_conv_p = jnp.stack(ffn_p)
    ffn_conv_s = jnp.stack(ffn_s)
    y_prompt = xp
    y_sample = xs
    return (y_prompt, y_sample,
            k_l0_p, v_l0_p, k_l1_p, v_l1_p, rg_h_p, rg_conv_p, k_l3_p, v_l3_p, ffn_conv_p,
            k_l0_s, v_l0_s, k_l1_s, v_l1_s, rg_h_s, rg_conv_s, k_l3_s, v_l3_s, ffn_conv_s)
```

```python
import functools
import math

import jax
import jax.numpy as jnp
from jax import lax
from jax.experimental import pallas as pl
from jax.experimental.pallas import tpu as pltpu

F32 = jnp.float32
BF16 = jnp.bfloat16

D_MODEL = 1024
DEPTH = 4
PAGE_SIZE = 128
HEAD_DIM = 64
DIFF_HEADS = D_MODEL // (2 * HEAD_DIM)
MOBA_HEADS = D_MODEL // HEAD_DIM
MOBA_BLOCK = 256
MOBA_TOPK = 3
RG_WIDTH = 3 * D_MODEL // 2
RG_BLOCKS = 12
RG_BW = RG_WIDTH // RG_BLOCKS
RG_CONV = 4
RG_C = 8.0
FFN_DIM = 2816
FFN_CONV = 3
ROPE_THETA = 10000.0
NORM_EPS = 1e-5
DEEPNORM_ALPHA = (2.0 * DEPTH) ** 0.25

V7X_LANES = 128
V7X_SUBLANES = 8
V7X_VMEM_BYTES = 64 * 1024 * 1024
VMEM_LIMIT = V7X_VMEM_BYTES * 7 // 8

NEG_BIG = -1e30

ROW_TILE = 512
FFN_CHUNK = 256
ATT_TQ = 512
ATT_TK = 512
MOBA_TQ = 512


def _params(*sem):
    return pltpu.CompilerParams(dimension_semantics=sem, vmem_limit_bytes=VMEM_LIMIT)


def _resident(shape):
    nd = len(shape)
    return pl.BlockSpec(shape, lambda *_: (0,) * nd, pipeline_mode=pl.Buffered(1))


def _layer_norm(z, g, b):
    mu = jnp.mean(z, axis=-1, keepdims=True)
    zc = z - mu
    var = jnp.mean(zc * zc, axis=-1, keepdims=True)
    return zc * lax.rsqrt(var + NORM_EPS) * g + b


def _shift_rows(a, halo, s):
    n = a.shape[0]
    rolled = pltpu.roll(a, s, axis=0)
    row = lax.broadcasted_iota(jnp.int32, a.shape, 0)
    out = rolled
    hn = halo.shape[0]
    for r in range(s):
        out = jnp.where(row == r, halo[hn - s + r:hn - s + r + 1, :], out)
    del n
    return out


def _rope_cols(t, cos, sa, sb):
    return t * cos + pltpu.roll(t, HEAD_DIM // 2, axis=1) * sa + pltpu.roll(t, V7X_LANES - HEAD_DIM // 2, axis=1) * sb


def _qkv_rope_kernel(x_ref, w_ref, cos_ref, sa_ref, sb_ref, q_ref, k_ref, v_ref, kb_ref, vb_ref, *rest,
                     q_scale, n_mean_blocks):
    xb = x_ref[...].astype(BF16)
    cos, sa, sb = cos_ref[...], sa_ref[...], sb_ref[...]
    d = D_MODEL
    q = jnp.dot(xb, w_ref[:, 0:d], preferred_element_type=F32)
    k = jnp.dot(xb, w_ref[:, d:2 * d], preferred_element_type=F32)
    v = jnp.dot(xb, w_ref[:, 2 * d:3 * d], preferred_element_type=F32)
    v_ref[...] = v
    vb_ref[...] = v.astype(BF16)
    for c in range(d // V7X_LANES):
        sl = slice(c * V7X_LANES, (c + 1) * V7X_LANES)
        qc = _rope_cols(q[:, sl], cos, sa, sb)
        kc = _rope_cols(k[:, sl], cos, sa, sb)
        q_ref[:, sl] = (qc * q_scale).astype(BF16)
        k_ref[:, sl] = kc
        kb_ref[:, sl] = kc.astype(BF16)
    if n_mean_blocks:
        kmean_ref = rest[0]
        for r in range(n_mean_blocks):
            blk = k_ref[r * MOBA_BLOCK:(r + 1) * MOBA_BLOCK, :]
            kmean_ref[r] = jnp.mean(blk, axis=0, keepdims=True)


def _rope_tables(pos):
    half = HEAD_DIM // 2
    inv_freq = ROPE_THETA ** (-jnp.arange(half, dtype=F32) / half)
    ang = pos.astype(F32)[:, None] * inv_freq[None, :]
    cos = jnp.cos(ang)
    sin = jnp.sin(ang)
    zero = jnp.zeros_like(sin)
    cos128 = jnp.tile(cos, (1, 4))
    sa128 = jnp.tile(jnp.concatenate([zero, sin], axis=1), (1, 2))
    sb128 = jnp.tile(jnp.concatenate([-sin, zero], axis=1), (1, 2))
    return cos128, sa128, sb128


def _qkv_rope(x, w, tables, *, tm, rows_per_seq, with_means):
    m = x.shape[0]
    nt = rows_per_seq // tm
    n_mean = tm // MOBA_BLOCK if with_means else 0
    row = pl.BlockSpec((tm, D_MODEL), lambda i: (i, 0))
    tab = pl.BlockSpec((tm, V7X_LANES), lambda i: (i % nt, 0))
    out_shape = [jax.ShapeDtypeStruct((m, D_MODEL), BF16), jax.ShapeDtypeStruct((m, D_MODEL), F32),
                 jax.ShapeDtypeStruct((m, D_MODEL), F32), jax.ShapeDtypeStruct((m, D_MODEL), BF16),
                 jax.ShapeDtypeStruct((m, D_MODEL), BF16)]
    out_specs = [row, row, row, row, row]
    if with_means:
        out_shape.append(jax.ShapeDtypeStruct((m // MOBA_BLOCK, 1, D_MODEL), F32))
        out_specs.append(pl.BlockSpec((n_mean, 1, D_MODEL), lambda i: (i, 0, 0)))
    return pl.pallas_call(
        functools.partial(_qkv_rope_kernel, q_scale=HEAD_DIM ** -0.5, n_mean_blocks=n_mean),
        grid=(m // tm,),
        in_specs=[row, _resident(w.shape), tab, tab, tab],
        out_specs=out_specs,
        out_shape=out_shape,
        compiler_params=_params("arbitrary"),
        name="qkv_rope",
    )(x, w, *tables)


def _proj_ln_kernel(o_ref, x_ref, w_ref, g_ref, b_ref, out_ref):
    y = jnp.dot(o_ref[...], w_ref[...], preferred_element_type=F32)
    out_ref[...] = _layer_norm(DEEPNORM_ALPHA * x_ref[...] + y, g_ref[...], b_ref[...])


def _proj_ln(o, x, w, g, b, *, tm):
    m, kd = o.shape
    row_o = pl.BlockSpec((tm, kd), lambda i: (i, 0))
    row_x = pl.BlockSpec((tm, D_MODEL), lambda i: (i, 0))
    vec = pl.BlockSpec((1, D_MODEL), lambda i: (0, 0))
    return pl.pallas_call(
        _proj_ln_kernel,
        grid=(m // tm,),
        in_specs=[row_o, row_x, _resident(w.shape), vec, vec],
        out_specs=row_x,
        out_shape=jax.ShapeDtypeStruct((m, D_MODEL), F32),
        compiler_params=_params("arbitrary"),
        name="proj_ln",
    )(o, x, w, g.reshape(1, -1), b.reshape(1, -1))


def _ffn_chunks(xb, xh, p_refs, wup_ref, wdn_ref, cw_ref, cb_ref, acc_ref, a_out_ref, write_tail):
    tm = xb.shape[0]
    acc_ref[...] = jnp.zeros_like(acc_ref)

    def chunk(c, carry):
        off = pl.multiple_of(c * FFN_CHUNK, FFN_CHUNK)
        wa = wup_ref[:, pl.ds(off, FFN_CHUNK)]
        wg = wup_ref[:, pl.ds(pl.multiple_of(FFN_DIM + off, V7X_LANES), FFN_CHUNK)]
        a = jnp.dot(xb, wa, preferred_element_type=F32)
        g = jnp.dot(xb, wg, preferred_element_type=F32)
        if xh is not None:
            ah = jnp.dot(xh, wa, preferred_element_type=F32)
            p1 = _shift_rows(a, ah, 1)
            p2 = _shift_rows(a, ah, 2)
        else:
            p2 = p_refs[0][:, pl.ds(off, FFN_CHUNK)]
            p1 = p_refs[1][:, pl.ds(off, FFN_CHUNK)]
        cw = cw_ref[:, pl.ds(off, FFN_CHUNK)]
        cb = cb_ref[:, pl.ds(off, FFN_CHUNK)]
        ac = cb + p2 * cw[0:1, :] + p1 * cw[1:2, :] + a * cw[2:3, :]
        h = (jax.nn.gelu(ac, approximate=True) * g).astype(BF16)
        acc_ref[...] += jnp.dot(h, wdn_ref[pl.ds(off, FFN_CHUNK), :], preferred_element_type=F32)
        if write_tail is None:
            a_out_ref[:, pl.ds(off, FFN_CHUNK)] = a
        else:
            @pl.when(write_tail)
            def _():
                a_out_ref[0, :, pl.ds(off, FFN_CHUNK)] = a[tm - V7X_SUBLANES:tm, :]
        return carry

    lax.fori_loop(0, FFN_DIM // FFN_CHUNK, chunk, 0)


def _ffn_seq_kernel(x_ref, xh_ref, wup_ref, wdn_ref, cw_ref, cb_ref, g_ref, b_ref, out_ref, tail_ref, acc_ref):
    i = pl.program_id(1)
    x = x_ref[...]
    xb = x.astype(BF16)
    xh = jnp.where(i == 0, 0.0, xh_ref[...]).astype(BF16)
    _ffn_chunks(xb, xh, None, wup_ref, wdn_ref, cw_ref, cb_ref, acc_ref, tail_ref, i == pl.num_programs(1) - 1)
    out_ref[...] = _layer_norm(DEEPNORM_ALPHA * x + acc_ref[...], g_ref[...], b_ref[...])


def _ffn_state_kernel(x_ref, p2_ref, p1_ref, wup_ref, wdn_ref, cw_ref, cb_ref, g_ref, b_ref, out_ref, a_ref, acc_ref):
    x = x_ref[...]
    _ffn_chunks(x.astype(BF16), None, (p2_ref, p1_ref), wup_ref, wdn_ref, cw_ref, cb_ref, acc_ref, a_ref, None)
    out_ref[...] = _layer_norm(DEEPNORM_ALPHA * x + acc_ref[...], g_ref[...], b_ref[...])


def _ffn_seq(x, wup, wdn, cw, cb, g, b, *, nb, tm):
    m = x.shape[0]
    nt = m // nb // tm
    hb = tm // V7X_SUBLANES
    row = pl.BlockSpec((tm, D_MODEL), lambda s, i: (s * nt + i, 0))
    halo = pl.BlockSpec((V7X_SUBLANES, D_MODEL), lambda s, i: (jnp.maximum((s * nt + i) * hb - 1, 0), 0))
    vec_d = pl.BlockSpec((1, D_MODEL), lambda s, i: (0, 0))
    return pl.pallas_call(
        _ffn_seq_kernel,
        grid=(nb, nt),
        in_specs=[row, halo, _resident(wup.shape), _resident(wdn.shape), _resident(cw.shape),
                  _resident((1, FFN_DIM)), vec_d, vec_d],
        out_specs=[row, pl.BlockSpec((1, V7X_SUBLANES, FFN_DIM), lambda s, i: (s, 0, 0))],
        out_shape=[jax.ShapeDtypeStruct((m, D_MODEL), F32),
                   jax.ShapeDtypeStruct((nb, V7X_SUBLANES, FFN_DIM), F32)],
        scratch_shapes=[pltpu.VMEM((tm, D_MODEL), F32)],
        compiler_params=_params("arbitrary", "arbitrary"),
        name="conv_ffn_seq",
    )(x, x, wup, wdn, cw, cb.reshape(1, -1), g.reshape(1, -1), b.reshape(1, -1))


def _ffn_state(x, p2, p1, wup, wdn, cw, cb, g, b):
    m = x.shape[0]
    full = lambda shape: pl.BlockSpec(shape, lambda i: (0,) * len(shape))
    return pl.pallas_call(
        _ffn_state_kernel,
        grid=(1,),
        in_specs=[full((m, D_MODEL)), full((m, FFN_DIM)), full((m, FFN_DIM)), _resident(wup.shape),
                  _resident(wdn.shape), _resident(cw.shape), _resident((1, FFN_DIM)),
                  full((1, D_MODEL)), full((1, D_MODEL))],
        out_specs=[full((m, D_MODEL)), full((m, FFN_DIM))],
        out_shape=[jax.ShapeDtypeStruct((m, D_MODEL), F32), jax.ShapeDtypeStruct((m, FFN_DIM), F32)],
        scratch_shapes=[pltpu.VMEM((m, D_MODEL), F32)],
        compiler_params=_params("arbitrary"),
        name="conv_ffn_state",
    )(x, p2, p1, wup, wdn, cw, cb.reshape(1, -1), g.reshape(1, -1), b.reshape(1, -1))


def _softmax_step(s, v, m_ref, l_ref, acc_ref):
    m_prev = m_ref[...]
    m_new = jnp.maximum(m_prev, jnp.max(s, axis=1, keepdims=True))
    alpha = jnp.exp(m_prev - m_new)
    p = jnp.exp(s - m_new)
    l_ref[...] = alpha * l_ref[...] + jnp.sum(p, axis=1, keepdims=True)
    acc_ref[...] = alpha * acc_ref[...] + jnp.dot(p.astype(BF16), v, preferred_element_type=F32)
    m_ref[...] = m_new


def _diff_lambda(lam_ref, lam_init):
    lp = lam_ref[...]
    s1 = jnp.sum(lp[0:1, :] * lp[1:2, :], axis=1, keepdims=True)
    s2 = jnp.sum(lp[2:3, :] * lp[3:4, :], axis=1, keepdims=True)
    return jnp.exp(s1) - jnp.exp(s2) + lam_init


def _diff_finish(o, subln, lam_init):
    ms = jnp.mean(o * o, axis=-1, keepdims=True)
    return o * lax.rsqrt(ms + NORM_EPS) * subln * (1.0 - lam_init)


def _diff_attn_kernel(lam_ref, subln_ref, q_ref, k_ref, v_ref, o_ref, m1, l1, a1, m2, l2, a2, *, tq, tk, lam_init):
    qi = pl.program_id(2)
    q = q_ref[0]
    lane = lax.broadcasted_iota(jnp.int32, q.shape, 1)
    q1 = jnp.where(lane < HEAD_DIM, q, jnp.zeros_like(q))
    q2 = jnp.where(lane >= HEAD_DIM, q, jnp.zeros_like(q))
    for m_, l_, a_ in ((m1, l1, a1), (m2, l2, a2)):
        m_[...] = jnp.full(m_.shape, NEG_BIG, F32)
        l_[...] = jnp.zeros(l_.shape, F32)
        a_[...] = jnp.zeros(a_.shape, F32)
    nt_dims = (((1,), (1,)), ((), ()))

    def step(j, masked):
        off = pl.multiple_of(j * tk, tk)
        kj = k_ref[0, pl.ds(off, tk), :]
        vj = v_ref[0, pl.ds(off, tk), :]
        if masked:
            qpos = qi * tq + lax.broadcasted_iota(jnp.int32, (tq, tk), 0)
            kpos = j * tk + lax.broadcasted_iota(jnp.int32, (tq, tk), 1)
            keep = kpos <= qpos
        for qq, m_, l_, a_ in ((q1, m1, l1, a1), (q2, m2, l2, a2)):
            s = lax.dot_general(qq, kj, nt_dims, preferred_element_type=F32)
            if masked:
                s = jnp.where(keep, s, NEG_BIG)
            _softmax_step(s, vj, m_, l_, a_)

    n_full = (qi * tq) // tk
    n_all = ((qi + 1) * tq) // tk
    lax.fori_loop(0, n_full, lambda j, c: (step(j, False), c)[1], 0)
    lax.fori_loop(n_full, n_all, lambda j, c: (step(j, True), c)[1], 0)

    lam = _diff_lambda(lam_ref, lam_init)
    o = a1[...] / l1[...] - lam * (a2[...] / l2[...])
    o_ref[0] = _diff_finish(o, subln_ref[...], lam_init).astype(BF16)


def _diff_attn(q, k, v, lam_p, subln, lam_init, *, tq, tk):
    nb, t, _ = q.shape
    qspec = pl.BlockSpec((1, tq, V7X_LANES), lambda b, h, i: (b, i, h))
    kvspec = pl.BlockSpec((1, t, V7X_LANES), lambda b, h, i: (b, 0, h))
    small = lambda shape: pl.BlockSpec(shape, lambda b, h, i: (0,) * len(shape))
    stat = pltpu.VMEM((tq, 1), F32)
    acc = pltpu.VMEM((tq, V7X_LANES), F32)
    return pl.pallas_call(
        functools.partial(_diff_attn_kernel, tq=tq, tk=tk, lam_init=lam_init),
        grid=(nb, DIFF_HEADS, t // tq),
        in_specs=[small((4, HEAD_DIM)), small((1, 2 * HEAD_DIM)), qspec, kvspec, kvspec],
        out_specs=qspec,
        out_shape=jax.ShapeDtypeStruct((nb, t, D_MODEL), BF16),
        scratch_shapes=[stat, stat, acc, stat, stat, acc],
        compiler_params=_params("arbitrary", "arbitrary", "arbitrary"),
        name="diff_attn",
    )(lam_p, subln.reshape(1, -1), q, k, v)


def _beats(row, g, tie_wins):
    return jnp.where(tie_wins, jnp.where(row >= g, 1.0, 0.0), jnp.where(row > g, 1.0, 0.0))


def _moba_select_bias(g_t, own, n_blocks):
    blk = lax.broadcasted_iota(jnp.int32, g_t.shape, 0)
    cnt = jnp.zeros(g_t.shape, F32)
    for n in range(n_blocks):
        cnt = cnt + _beats(g_t[n:n + 1, :], g_t, blk > n) * jnp.where(n < own, 1.0, 0.0)
    visible = jnp.where(cnt < MOBA_TOPK, 1.0, 0.0) * jnp.where(blk < own, 1.0, 0.0) + jnp.where(blk == own, 1.0, 0.0)
    return jnp.where(visible > 0.5, 0.0, NEG_BIG)


def _moba_attn_kernel(q_ref, k_ref, v_ref, mean_ref, o_ref, m1, l1, a1, m2, l2, a2, sel1, sel2, *, tq, tk, n_blocks):
    qi = pl.program_id(2)
    q = q_ref[0]
    lane = lax.broadcasted_iota(jnp.int32, q.shape, 1)
    q1 = jnp.where(lane < HEAD_DIM, q, jnp.zeros_like(q))
    q2 = jnp.where(lane >= HEAD_DIM, q, jnp.zeros_like(q))
    nt_dims = (((1,), (1,)), ((), ()))
    means = mean_ref[0].astype(BF16)
    own = (qi * tq + lax.broadcasted_iota(jnp.int32, (1, tq), 1)) // MOBA_BLOCK
    eye = (lax.broadcasted_iota(jnp.int32, (tq, tq), 0) == lax.broadcasted_iota(jnp.int32, (tq, tq), 1))
    eye = jnp.where(eye, 1.0, 0.0).astype(BF16)
    pad = jnp.zeros((V7X_LANES - n_blocks, tq), BF16)
    for qq, m_, l_, a_, sel in ((q1, m1, l1, a1, sel1), (q2, m2, l2, a2, sel2)):
        m_[...] = jnp.full(m_.shape, NEG_BIG, F32)
        l_[...] = jnp.zeros(l_.shape, F32)
        a_[...] = jnp.zeros(a_.shape, F32)
        g_t = lax.dot_general(means, qq, nt_dims, preferred_element_type=F32)
        bias_t = jnp.concatenate([_moba_select_bias(g_t, own, n_blocks).astype(BF16), pad], axis=0)
        sel[...] = lax.dot_general(eye, bias_t, nt_dims, preferred_element_type=F32).astype(BF16)

    def step(j, masked):
        off = pl.multiple_of(j * tk, tk)
        kj = k_ref[0, pl.ds(off, tk), :]
        vj = v_ref[0, pl.ds(off, tk), :]
        onehot = jnp.where(lax.broadcasted_iota(jnp.int32, (V7X_LANES, tk), 0) == j, 1.0, 0.0).astype(BF16)
        if masked:
            qpos = qi * tq + lax.broadcasted_iota(jnp.int32, (tq, tk), 0)
            kpos = j * tk + lax.broadcasted_iota(jnp.int32, (tq, tk), 1)
            keep = kpos <= qpos
        for qq, m_, l_, a_, sel in ((q1, m1, l1, a1, sel1), (q2, m2, l2, a2, sel2)):
            s = lax.dot_general(qq, kj, nt_dims, preferred_element_type=F32)
            s = s + jnp.dot(sel[...], onehot, preferred_element_type=F32)
            if masked:
                s = jnp.where(keep, s, NEG_BIG)
            _softmax_step(s, vj, m_, l_, a_)

    n_full = (qi * tq) // tk
    n_all = ((qi + 1) * tq) // tk
    lax.fori_loop(0, n_full, lambda j, c: (step(j, False), c)[1], 0)
    lax.fori_loop(n_full, n_all, lambda j, c: (step(j, True), c)[1], 0)
    o_ref[0] = jnp.where(lane < HEAD_DIM, a1[...] / l1[...], a2[...] / l2[...]).astype(BF16)


def _moba_attn(q, k, v, kmeans, *, tq):
    nb, t, _ = q.shape
    n_blocks = t // MOBA_BLOCK
    qspec = pl.BlockSpec((1, tq, V7X_LANES), lambda b, h, i: (b, i, h))
    kvspec = pl.BlockSpec((1, t, V7X_LANES), lambda b, h, i: (b, 0, h))
    mspec = pl.BlockSpec((1, n_blocks, V7X_LANES), lambda b, h, i: (b, 0, h))
    stat = pltpu.VMEM((tq, 1), F32)
    acc = pltpu.VMEM((tq, V7X_LANES), F32)
    sel = pltpu.VMEM((tq, V7X_LANES), BF16)
    return pl.pallas_call(
        functools.partial(_moba_attn_kernel, tq=tq, tk=MOBA_BLOCK, n_blocks=n_blocks),
        grid=(nb, MOBA_HEADS // 2, t // tq),
        in_specs=[qspec, kvspec, kvspec, mspec],
        out_specs=qspec,
        out_shape=jax.ShapeDtypeStruct((nb, t, D_MODEL), BF16),
        scratch_shapes=[stat, stat, acc, stat, stat, acc, sel, sel],
        compiler_params=_params("arbitrary", "arbitrary", "arbitrary"),
        name="moba_attn",
    )(q, k, v, kmeans)


def _softplus(z):
    return jnp.maximum(z, 0.0) + jnp.log1p(jnp.exp(-jnp.abs(z)))


def _rg_gates(uc, wa_ref, ba_ref, wi_ref, bi_ref, lam_ref, a_ref, b_ref):
    sp = _softplus(-lam_ref[...])
    for n in range(RG_BLOCKS):
        sl = slice(n * RG_BW, (n + 1) * RG_BW)
        ub = uc[:, sl]
        ubb = ub.astype(BF16)
        r = jax.nn.sigmoid(jnp.dot(ubb, wa_ref[n], preferred_element_type=F32) + ba_ref[:, sl])
        ig = jax.nn.sigmoid(jnp.dot(ubb, wi_ref[n], preferred_element_type=F32) + bi_ref[:, sl])
        log_a = -RG_C * r * sp[:, sl]
        a = jnp.exp(log_a)
        a_ref[:, sl] = a
        b_ref[:, sl] = jnp.sqrt(1.0 - jnp.exp(2.0 * log_a)) * (ig * ub)


def _rg_seq_kernel(x_ref, xh_ref, win_ref, cw_ref, cb_ref, wa_ref, ba_ref, wi_ref, bi_ref, lam_ref,
                   y_ref, hl_ref, tail_ref, gate_s, a_s, b_s, h_s, carry_s):
    i = pl.program_id(1)
    tm = x_ref.shape[0]
    w = RG_WIDTH
    xb = x_ref[...].astype(BF16)
    xh = jnp.where(i == 0, 0.0, xh_ref[...]).astype(BF16)
    gate_s[...] = jnp.dot(xb, win_ref[:, 0:w], preferred_element_type=F32)
    u = jnp.dot(xb, win_ref[:, w:2 * w], preferred_element_type=F32)
    uh = jnp.dot(xh, win_ref[:, w:2 * w], preferred_element_type=F32)
    cw = cw_ref[...]
    uc = cb_ref[...] + u * cw[3:4, :]
    for s in range(1, RG_CONV):
        uc = uc + _shift_rows(u, uh, s) * cw[3 - s:4 - s, :]
    _rg_gates(uc, wa_ref, ba_ref, wi_ref, bi_ref, lam_ref, a_s, b_s)

    @pl.when(i == 0)
    def _():
        carry_s[...] = jnp.zeros_like(carry_s)

    row = lax.broadcasted_iota(jnp.int32, (V7X_SUBLANES, w), 0)

    def group(gi, c):
        off = pl.multiple_of(gi * V7X_SUBLANES, V7X_SUBLANES)
        a8 = a_s[pl.ds(off, V7X_SUBLANES), :]
        b8 = b_s[pl.ds(off, V7X_SUBLANES), :]
        for s in (1, 2, 4):
            ok = row >= s
            b8 = jnp.where(ok, a8 * pltpu.roll(b8, s, axis=0) + b8, b8)
            a8 = jnp.where(ok, a8 * pltpu.roll(a8, s, axis=0), a8)
        h8 = a8 * carry_s[V7X_SUBLANES - 1:V7X_SUBLANES, :] + b8
        h_s[pl.ds(off, V7X_SUBLANES), :] = h8
        carry_s[...] = h8
        return c

    lax.fori_loop(0, tm // V7X_SUBLANES, group, 0)
    y_ref[...] = (jax.nn.gelu(gate_s[...], approximate=True) * h_s[...]).astype(BF16)

    @pl.when(i == pl.num_programs(1) - 1)
    def _():
        hl_ref[0] = carry_s[...]
        tail_ref[0] = u[tm - V7X_SUBLANES:tm, :]


def _rg_seq(x, win, cw, cb, wa, ba, wi, bi, lam, *, nb, tm):
    m = x.shape[0]
    nt = m // nb // tm
    hb = tm // V7X_SUBLANES
    w = RG_WIDTH
    row = pl.BlockSpec((tm, D_MODEL), lambda s, i: (s * nt + i, 0))
    halo = pl.BlockSpec((V7X_SUBLANES, D_MODEL), lambda s, i: (jnp.maximum((s * nt + i) * hb - 1, 0), 0))
    vec = _resident((1, w))
    per_seq = pl.BlockSpec((1, V7X_SUBLANES, w), lambda s, i: (s, 0, 0))
    big = pltpu.VMEM((tm, w), F32)
    return pl.pallas_call(
        _rg_seq_kernel,
        grid=(nb, nt),
        in_specs=[row, halo, _resident(win.shape), _resident(cw.shape), vec, _resident(wa.shape), vec,
                  _resident(wi.shape), vec, vec],
        out_specs=[pl.BlockSpec((tm, w), lambda s, i: (s * nt + i, 0)), per_seq, per_seq],
        out_shape=[jax.ShapeDtypeStruct((m, w), BF16), jax.ShapeDtypeStruct((nb, V7X_SUBLANES, w), F32),
                   jax.ShapeDtypeStruct((nb, V7X_SUBLANES, w), F32)],
        scratch_shapes=[big, big, big, big, pltpu.VMEM((V7X_SUBLANES, w), F32)],
        compiler_params=_params("arbitrary", "arbitrary"),
        name="rglru_seq",
    )(x, x, win, cw, cb.reshape(1, -1), wa, ba.reshape(1, -1), wi, bi.reshape(1, -1), lam.reshape(1, -1))


def _rg_state_kernel(x_ref, h0_ref, c0_ref, c1_ref, c2_ref, win_ref, cw_ref, cb_ref, wa_ref, ba_ref, wi_ref, bi_ref,
                     lam_ref, y_ref, h_ref, u_ref, a_s, b_s):
    w = RG_WIDTH
    xb = x_ref[...].astype(BF16)
    gate = jnp.dot(xb, win_ref[:, 0:w], preferred_element_type=F32)
    u = jnp.dot(xb, win_ref[:, w:2 * w], preferred_element_type=F32)
    cw = cw_ref[...]
    uc = cb_ref[...] + c0_ref[...] * cw[0:1, :] + c1_ref[...] * cw[1:2, :] + c2_ref[...] * cw[2:3, :] + u * cw[3:4, :]
    _rg_gates(uc, wa_ref, ba_ref, wi_ref, bi_ref, lam_ref, a_s, b_s)
    h = a_s[...] * h0_ref[...] + b_s[...]
    h_ref[...] = h
    u_ref[...] = u
    y_ref[...] = (jax.nn.gelu(gate, approximate=True) * h).astype(BF16)


def _rg_state(x, h0, c0, c1, c2, win, cw, cb, wa, ba, wi, bi, lam):
    m = x.shape[0]
    w = RG_WIDTH
    full = lambda shape: pl.BlockSpec(shape, lambda i: (0,) * len(shape))
    st = full((m, w))
    vec = full((1, w))
    return pl.pallas_call(
        _rg_state_kernel,
        grid=(1,),
        in_specs=[full((m, D_MODEL)), st, st, st, st, _resident(win.shape), full(cw.shape), vec,
                  _resident(wa.shape), vec, _resident(wi.shape), vec, vec],
        out_specs=[st, st, st],
        out_shape=[jax.ShapeDtypeStruct((m, w), BF16), jax.ShapeDtypeStruct((m, w), F32),
                   jax.ShapeDtypeStruct((m, w), F32)],
        scratch_shapes=[pltpu.VMEM((m, w), F32), pltpu.VMEM((m, w), F32)],
        compiler_params=_params("arbitrary"),
        name="rglru_state",
    )(x, h0, c0, c1, c2, win, cw, cb.reshape(1, -1), wa, ba.reshape(1, -1), wi, bi.reshape(1, -1), lam.reshape(1, -1))


def _page_scores(kp, q, width):
    prod = (kp * q[None]).reshape(PAGE_SIZE * MOBA_HEADS, HEAD_DIM).astype(BF16)
    ones = jnp.ones((HEAD_DIM, width), BF16)
    return jnp.dot(prod, ones, preferred_element_type=F32)


def _diff_decode_kernel(pt_ref, lam_ref, subln_ref, q_ref, kn_ref, vn_ref, kc_ref, vc_ref, o_ref,
                        s_s, m1, l1, a1, m2, l2, a2, *, lam_init):
    del pt_ref
    p = pl.program_id(1)
    h = DIFF_HEADS
    q = q_ref[0]

    @pl.when(p == 0)
    def _():
        s_s[0:2 * h, :] = jnp.broadcast_to(jnp.sum(kn_ref[0] * q, axis=-1, keepdims=True), (2 * h, V7X_LANES))
        for j, (m_, l_, a_) in enumerate(((m1, l1, a1), (m2, l2, a2))):
            m_[...] = s_s[pl.ds(j, h, stride=2), :]
            l_[...] = jnp.ones(l_.shape, F32)
            a_[...] = vn_ref[0]

    s_s[...] = _page_scores(kc_ref[0], q, V7X_LANES)
    vp = vc_ref[0]
    for j, (m_, l_, a_) in enumerate(((m1, l1, a1), (m2, l2, a2))):
        s = s_s[pl.ds(j, PAGE_SIZE * h, stride=2), :].reshape(PAGE_SIZE, h, V7X_LANES)
        m_prev = m_[...]
        m_new = jnp.maximum(m_prev, jnp.max(s, axis=0))
        alpha = jnp.exp(m_prev - m_new)
        pr = jnp.exp(s - m_new[None])
        l_[...] = alpha * l_[...] + jnp.sum(pr, axis=0)
        a_[...] = alpha * a_[...] + jnp.sum(pr * vp, axis=0)
        m_[...] = m_new

    @pl.when(p == pl.num_programs(1) - 1)
    def _():
        lam = _diff_lambda(lam_ref, lam_init)
        o = a1[...] / l1[...] - lam * (a2[...] / l2[...])
        o_ref[0] = _diff_finish(o, subln_ref[...], lam_init).astype(BF16)


def _diff_decode(q, k_new, v_new, cache_k, cache_v, page_table, lam_p, subln, lam_init):
    nb, n_pages = page_table.shape
    h = DIFF_HEADS
    c = 2 * h
    per_seq = lambda shape: pl.BlockSpec((1,) + shape, lambda b, p, pt: (b,) + (0,) * len(shape))
    small = lambda shape: pl.BlockSpec(shape, lambda b, p, pt: (0,) * len(shape))
    kspec = pl.BlockSpec((1, PAGE_SIZE, c, HEAD_DIM), lambda b, p, pt: (pt[b, p], 0, 0, 0))
    vspec = pl.BlockSpec((1, PAGE_SIZE, h, 2 * HEAD_DIM), lambda b, p, pt: (pt[b, p], 0, 0, 0))
    stat = pltpu.VMEM((h, V7X_LANES), F32)
    return pl.pallas_call(
        functools.partial(_diff_decode_kernel, lam_init=lam_init),
        grid_spec=pltpu.PrefetchScalarGridSpec(
            num_scalar_prefetch=1,
            grid=(nb, n_pages),
            in_specs=[small((4, HEAD_DIM)), small((1, 2 * HEAD_DIM)), per_seq((c, HEAD_DIM)), per_seq((c, HEAD_DIM)),
                      per_seq((h, 2 * HEAD_DIM)), kspec, vspec],
            out_specs=per_seq((h, 2 * HEAD_DIM)),
            scratch_shapes=[pltpu.VMEM((PAGE_SIZE * c, V7X_LANES), F32), stat, stat, stat, stat, stat, stat],
        ),
        out_shape=jax.ShapeDtypeStruct((nb, h, 2 * HEAD_DIM), BF16),
        compiler_params=_params("arbitrary", "arbitrary"),
        name="diff_decode",
    )(page_table, lam_p, subln.reshape(1, -1), q, k_new, v_new, cache_k, cache_v)


def _moba_decode_kernel(pt_ref, q_ref, kn_ref, vn_ref, kc_ref, vc_ref, o_ref, m_s, l_s, a_s, ks_s, *, n_pages):
    del pt_ref
    p = pl.program_id(1)
    c = MOBA_HEADS
    q = q_ref[0]
    kp = kc_ref[0]
    s = _page_scores(kp, q, HEAD_DIM).reshape(PAGE_SIZE, c, HEAD_DIM)
    m = jnp.max(s, axis=0)
    pr = jnp.exp(s - m[None])
    m_s[p] = m
    l_s[p] = jnp.sum(pr, axis=0)
    a_s[p] = jnp.sum(pr * vc_ref[0], axis=0)
    ks_s[p] = jnp.sum(kp, axis=0)

    @pl.when(p == n_pages - 1)
    def _():
        pages_per_block = MOBA_BLOCK // PAGE_SIZE
        nblk = n_pages // pages_per_block
        ksum = jnp.sum(ks_s[...].reshape(nblk, pages_per_block, c, HEAD_DIM), axis=1)
        means = ksum * (1.0 / MOBA_BLOCK)
        gate = jnp.broadcast_to(jnp.sum(means * q[None], axis=-1, keepdims=True), means.shape)
        cnt = jnp.zeros(gate.shape, F32)
        blk = lax.broadcasted_iota(jnp.int32, gate.shape, 0)
        for n in range(nblk):
            cnt = cnt + _beats(gate[n:n + 1], gate, blk > n)
        selw = jnp.where(cnt < MOBA_TOPK, 1.0, 0.0)
        selw = jnp.broadcast_to(selw[:, None], (nblk, pages_per_block, c, HEAD_DIM)).reshape(n_pages, c, HEAD_DIM)
        sel = selw > 0.5
        s_new = jnp.broadcast_to(jnp.sum(kn_ref[0] * q, axis=-1, keepdims=True), (c, HEAD_DIM))
        mp = jnp.where(sel, m_s[...], NEG_BIG)
        m_all = jnp.maximum(jnp.max(mp, axis=0), s_new)
        w = jnp.where(sel, jnp.exp(mp - m_all[None]), 0.0)
        w_new = jnp.exp(s_new - m_all)
        num = jnp.sum(w * a_s[...], axis=0) + w_new * vn_ref[0]
        den = jnp.sum(w * l_s[...], axis=0) + w_new
        o_ref[0] = (num / den).astype(BF16)


def _moba_decode(q, k_new, v_new, cache_k, cache_v, page_table):
    nb, n_pages = page_table.shape
    c = MOBA_HEADS
    per_seq = pl.BlockSpec((1, c, HEAD_DIM), lambda b, p, pt: (b, 0, 0))
    page = pl.BlockSpec((1, PAGE_SIZE, c, HEAD_DIM), lambda b, p, pt: (pt[b, p], 0, 0, 0))
    part = pltpu.VMEM((n_pages, c, HEAD_DIM), F32)
    return pl.pallas_call(
        functools.partial(_moba_decode_kernel, n_pages=n_pages),
        grid_spec=pltpu.PrefetchScalarGridSpec(
            num_scalar_prefetch=1,
            grid=(nb, n_pages),
            in_specs=[per_seq, per_seq, per_seq, page, page],
            out_specs=per_seq,
            scratch_shapes=[part, part, part, part],
        ),
        out_shape=jax.ShapeDtypeStruct((nb, c, HEAD_DIM), BF16),
        compiler_params=_params("arbitrary", "arbitrary"),
        name="moba_decode",
    )(page_table, q, k_new, v_new, cache_k, cache_v)


def kernel(x_prompt, x_sample, cache_k_l0, cache_v_l0, cache_k_l1, cache_v_l1, cache_k_l3, cache_v_l3, state_rg_h_l2, state_rg_conv_l2, state_ffn_conv, page_table, l0_w_qkv, l0_lambda, l0_subln, l0_w_o, l1_w_qkv, l1_w_o, l2_w_in, l2_conv_w, l2_conv_b, l2_w_a, l2_b_a, l2_w_i, l2_b_i, l2_lambda, l2_w_o, l3_w_qkv, l3_lambda, l3_subln, l3_w_o, ffn_w_up, ffn_conv_w, ffn_conv_b, ffn_w_down, ln_mix_g, ln_mix_b, ln_ffn_g, ln_ffn_b):
    nb, t, d = x_prompt.shape
    ns = x_sample.shape[0]
    past_len = page_table.shape[1] * PAGE_SIZE
    tm = min(ROW_TILE, t)
    tab_p = _rope_tables(jnp.arange(t, dtype=jnp.int32))
    tab_s = _rope_tables(jnp.full((ns,), past_len, dtype=jnp.int32))
    xp = x_prompt.reshape(nb * t, d)
    xs = x_sample.reshape(ns, d)
    outs_p, outs_s, ffn_p, ffn_s = {}, {}, [], []

    def attn_layer(i, w_qkv, w_o, cache_k, cache_v, lam_p=None, subln=None):
        nonlocal xp, xs
        wq = w_qkv.astype(BF16)
        wo = w_o.astype(BF16)
        diff = lam_p is not None
        res_p = _qkv_rope(xp, wq, tab_p, tm=tm, rows_per_seq=t, with_means=not diff)
        q_p, k_p, v_p, kb_p, vb_p = res_p[:5]
        q_s, k_s, v_s, _, _ = _qkv_rope(xs, wq, tab_s, tm=ns, rows_per_seq=ns, with_means=False)
        shp = lambda a: a.reshape(nb, t, d)
        q_s3 = q_s.astype(F32).reshape(ns, 2 * DIFF_HEADS, HEAD_DIM)
        k_s3 = k_s.reshape(ns, 2 * DIFF_HEADS, HEAD_DIM)
        if diff:
            lam_init = 0.8 - 0.6 * math.exp(-0.3 * i)
            o_p = _diff_attn(shp(q_p), shp(kb_p), shp(vb_p), lam_p, subln, lam_init, tq=min(ATT_TQ, t), tk=min(ATT_TK, t))
            v_s3 = v_s.reshape(ns, DIFF_HEADS, 2 * HEAD_DIM)
            o_s = _diff_decode(q_s3, k_s3, v_s3, cache_k, cache_v, page_table, lam_p, subln, lam_init)
            kv_shape = ((2 * DIFF_HEADS, HEAD_DIM), (DIFF_HEADS, 2 * HEAD_DIM))
        else:
            kmeans = res_p[5].reshape(nb, t // MOBA_BLOCK, d)
            o_p = _moba_attn(shp(q_p), shp(kb_p), shp(vb_p), kmeans, tq=min(MOBA_TQ, t))
            v_s3 = v_s.reshape(ns, MOBA_HEADS, HEAD_DIM)
            o_s = _moba_decode(q_s3, k_s3, v_s3, cache_k, cache_v, page_table)
            kv_shape = ((MOBA_HEADS, HEAD_DIM), (MOBA_HEADS, HEAD_DIM))
        outs_p[i] = (k_p.reshape(nb, t, *kv_shape[0]), v_p.reshape(nb, t, *kv_shape[1]))
        outs_s[i] = (k_s.reshape(ns, 1, *kv_shape[0]), v_s.reshape(ns, 1, *kv_shape[1]))
        xp = _proj_ln(o_p.reshape(nb * t, d), xp, wo, ln_mix_g[i], ln_mix_b[i], tm=tm)
        xs = _proj_ln(o_s.reshape(ns, d), xs, wo, ln_mix_g[i], ln_mix_b[i], tm=ns)

    def rg_layer(i):
        nonlocal xp, xs
        win = l2_w_in.astype(BF16)
        wa = l2_w_a.astype(BF16)
        wi = l2_w_i.astype(BF16)
        wo = l2_w_o.astype(BF16)
        y_p, hl_p, tail_p = _rg_seq(xp, win, l2_conv_w, l2_conv_b, wa, l2_b_a, wi, l2_b_i, l2_lambda, nb=nb, tm=tm)
        cs = state_rg_conv_l2
        y_s, h_s, u_s = _rg_state(xs, state_rg_h_l2, cs[:, 0], cs[:, 1], cs[:, 2], win, l2_conv_w, l2_conv_b,
                                  wa, l2_b_a, wi, l2_b_i, l2_lambda)
        outs_p[i] = (hl_p[:, V7X_SUBLANES - 1], tail_p[:, V7X_SUBLANES - (RG_CONV - 1):])
        outs_s[i] = (h_s, jnp.concatenate([cs[:, 1:], u_s[:, None]], axis=1))
        xp = _proj_ln(y_p, xp, wo, ln_mix_g[i], ln_mix_b[i], tm=tm)
        xs = _proj_ln(y_s, xs, wo, ln_mix_g[i], ln_mix_b[i], tm=ns)

    def ffn_layer(i):
        nonlocal xp, xs
        wup = ffn_w_up[i].astype(BF16)
        wdn = ffn_w_down[i].astype(BF16)
        xp, tail = _ffn_seq(xp, wup, wdn, ffn_conv_w[i], ffn_conv_b[i], ln_ffn_g[i], ln_ffn_b[i], nb=nb, tm=tm)
        st = state_ffn_conv[i]
        xs, a_s = _ffn_state(xs, st[:, 0], st[:, 1], wup, wdn, ffn_conv_w[i], ffn_conv_b[i], ln_ffn_g[i], ln_ffn_b[i])
        ffn_p.append(tail[:, V7X_SUBLANES - (FFN_CONV - 1):])
        ffn_s.append(jnp.concatenate([st[:, 1:], a_s[:, None]], axis=1))

    attn_layer(0, l0_w_qkv, l0_w_o, cache_k_l0, cache_v_l0, l0_lambda, l0_subln)
    ffn_layer(0)
    attn_layer(1, l1_w_qkv, l1_w_o, cache_k_l1, cache_v_l1)
    ffn_layer(1)
    rg_layer(2)
    ffn_layer(2)
    attn_layer(3, l3_w_qkv, l3_w_o, cache_k_l3, cache_v_l3, l3_lambda, l3_subln)
    ffn_layer(3)

    return (xp.reshape(nb, t, d), xs.reshape(ns, 1, d),
            outs_p[0][0], outs_p[0][1], outs_p[1][0], outs_p[1][1], outs_p[2][0], outs_p[2][1],
            outs_p[3][0], outs_p[3][1], jnp.stack(ffn_p),
            outs_s[0][0], outs_s[0][1], outs_s[1][0], outs_s[1][1], outs_s[2][0], outs_s[2][1],
            outs_s[3][0], outs_s[3][1], jnp.stack(ffn_s))
```

```python
import functools
import math

import jax
import jax.numpy as jnp
from jax import lax
from jax.experimental import pallas as pl
from jax.experimental.pallas import tpu as pltpu

F32 = jnp.float32
BF16 = jnp.bfloat16

D_MODEL = 1024
DEPTH = 4
PAGE_SIZE = 128
HEAD_DIM = 64
DIFF_HEADS = D_MODEL // (2 * HEAD_DIM)
MOBA_HEADS = D_MODEL // HEAD_DIM
MOBA_BLOCK = 256
MOBA_TOPK = 3
RG_WIDTH = 3 * D_MODEL // 2
RG_BLOCKS = 12
RG_BW = RG_WIDTH // RG_BLOCKS
RG_CONV = 4
RG_C = 8.0
FFN_DIM = 2816
FFN_CONV = 3
ROPE_THETA = 10000.0
NORM_EPS = 1e-5
DEEPNORM_ALPHA = (2.0 * DEPTH) ** 0.25

V7X_LANES = 128
V7X_SUBLANES = 8
V7X_VMEM_BYTES = 64 * 1024 * 1024
VMEM_LIMIT = V7X_VMEM_BYTES * 7 // 8

NEG_BIG = -1e30

ROW_TILE = 512
FFN_ROW_TILE = 1024
FFN_CHUNK = 256
ATT_TQ = 512
ATT_TK = 512
DECODE_PAGES = 8


def _params(*sem):
    return pltpu.CompilerParams(dimension_semantics=sem, vmem_limit_bytes=VMEM_LIMIT)


def _resident(shape):
    nd = len(shape)
    return pl.BlockSpec(shape, lambda *_: (0,) * nd, pipeline_mode=pl.Buffered(1))


def _layer_norm(z, g, b):
    mu = jnp.mean(z, axis=-1, keepdims=True)
    zc = z - mu
    var = jnp.mean(zc * zc, axis=-1, keepdims=True)
    return zc * lax.rsqrt(var + NORM_EPS) * g + b


def _shift_rows(a, halo, s):
    row = lax.broadcasted_iota(jnp.int32, a.shape, 0)
    out = pltpu.roll(a, s, axis=0)
    hn = halo.shape[0]
    for r in range(s):
        out = jnp.where(row == r, halo[hn - s + r:hn - s + r + 1, :], out)
    return out


def _rope_cols(t, cos, sa, sb):
    return t * cos + pltpu.roll(t, HEAD_DIM // 2, axis=1) * sa + pltpu.roll(t, V7X_LANES - HEAD_DIM // 2, axis=1) * sb


def _qkv_rope_kernel(x_ref, w_ref, cos_ref, sa_ref, sb_ref, q_ref, k_ref, v_ref, kb_ref, vb_ref, *rest,
                     q_scale, n_mean_blocks):
    xb = x_ref[...].astype(BF16)
    cos, sa, sb = cos_ref[...], sa_ref[...], sb_ref[...]
    d = D_MODEL
    q = jnp.dot(xb, w_ref[:, 0:d], preferred_element_type=F32)
    k = jnp.dot(xb, w_ref[:, d:2 * d], preferred_element_type=F32)
    v = jnp.dot(xb, w_ref[:, 2 * d:3 * d], preferred_element_type=F32)
    v_ref[...] = v
    vb_ref[...] = v.astype(BF16)
    for c in range(d // V7X_LANES):
        sl = slice(c * V7X_LANES, (c + 1) * V7X_LANES)
        qc = _rope_cols(q[:, sl], cos, sa, sb)
        kc = _rope_cols(k[:, sl], cos, sa, sb)
        q_ref[:, sl] = (qc * q_scale).astype(BF16)
        k_ref[:, sl] = kc
        kb_ref[:, sl] = kc.astype(BF16)
    if n_mean_blocks:
        kmean_ref = rest[0]
        for r in range(n_mean_blocks):
            blk = k_ref[r * MOBA_BLOCK:(r + 1) * MOBA_BLOCK, :]
            kmean_ref[r] = jnp.mean(blk, axis=0, keepdims=True)


def _rope_tables(pos):
    half = HEAD_DIM // 2
    inv_freq = ROPE_THETA ** (-jnp.arange(half, dtype=F32) / half)
    ang = pos.astype(F32)[:, None] * inv_freq[None, :]
    cos = jnp.cos(ang)
    sin = jnp.sin(ang)
    zero = jnp.zeros_like(sin)
    cos128 = jnp.tile(cos, (1, 4))
    sa128 = jnp.tile(jnp.concatenate([zero, sin], axis=1), (1, 2))
    sb128 = jnp.tile(jnp.concatenate([-sin, zero], axis=1), (1, 2))
    return cos128, sa128, sb128


def _qkv_rope(x, w, tables, *, tm, rows_per_seq, with_means):
    m = x.shape[0]
    nt = rows_per_seq // tm
    n_mean = tm // MOBA_BLOCK if with_means else 0
    row = pl.BlockSpec((tm, D_MODEL), lambda i: (i, 0))
    tab = pl.BlockSpec((tm, V7X_LANES), lambda i: (i % nt, 0))
    out_shape = [jax.ShapeDtypeStruct((m, D_MODEL), BF16), jax.ShapeDtypeStruct((m, D_MODEL), F32),
                 jax.ShapeDtypeStruct((m, D_MODEL), F32), jax.ShapeDtypeStruct((m, D_MODEL), BF16),
                 jax.ShapeDtypeStruct((m, D_MODEL), BF16)]
    out_specs = [row, row, row, row, row]
    if with_means:
        out_shape.append(jax.ShapeDtypeStruct((m // MOBA_BLOCK, 1, D_MODEL), F32))
        out_specs.append(pl.BlockSpec((n_mean, 1, D_MODEL), lambda i: (i, 0, 0)))
    return pl.pallas_call(
        functools.partial(_qkv_rope_kernel, q_scale=HEAD_DIM ** -0.5, n_mean_blocks=n_mean),
        grid=(m // tm,),
        in_specs=[row, _resident(w.shape), tab, tab, tab],
        out_specs=out_specs,
        out_shape=out_shape,
        compiler_params=_params("arbitrary"),
        name="qkv_rope",
    )(x, w, *tables)


def _proj_ln_kernel(o_ref, x_ref, w_ref, g_ref, b_ref, out_ref):
    y = jnp.dot(o_ref[...], w_ref[...], preferred_element_type=F32)
    out_ref[...] = _layer_norm(DEEPNORM_ALPHA * x_ref[...] + y, g_ref[...], b_ref[...])


def _proj_ln(o, x, w, g, b, *, tm):
    m, kd = o.shape
    row_o = pl.BlockSpec((tm, kd), lambda i: (i, 0))
    row_x = pl.BlockSpec((tm, D_MODEL), lambda i: (i, 0))
    vec = pl.BlockSpec((1, D_MODEL), lambda i: (0, 0))
    return pl.pallas_call(
        _proj_ln_kernel,
        grid=(m // tm,),
        in_specs=[row_o, row_x, _resident(w.shape), vec, vec],
        out_specs=row_x,
        out_shape=jax.ShapeDtypeStruct((m, D_MODEL), F32),
        compiler_params=_params("arbitrary"),
        name="proj_ln",
    )(o, x, w, g.reshape(1, -1), b.reshape(1, -1))


def _ffn_chunks(xb, xh, p_refs, wup_ref, wdn_ref, cw_ref, cb_ref, acc_ref, a_out_ref, write_tail):
    tm = xb.shape[0]
    acc_ref[...] = jnp.zeros_like(acc_ref)

    def chunk(c, carry):
        off = pl.multiple_of(c * FFN_CHUNK, FFN_CHUNK)
        wa = wup_ref[:, pl.ds(off, FFN_CHUNK)]
        wg = wup_ref[:, pl.ds(pl.multiple_of(FFN_DIM + off, V7X_LANES), FFN_CHUNK)]
        a = jnp.dot(xb, wa, preferred_element_type=F32)
        g = jnp.dot(xb, wg, preferred_element_type=F32)
        if xh is not None:
            ah = jnp.dot(xh, wa, preferred_element_type=F32)
            p1 = _shift_rows(a, ah, 1)
            p2 = _shift_rows(a, ah, 2)
        else:
            p2 = p_refs[0][:, pl.ds(off, FFN_CHUNK)]
            p1 = p_refs[1][:, pl.ds(off, FFN_CHUNK)]
        cw = cw_ref[:, pl.ds(off, FFN_CHUNK)]
        cb = cb_ref[:, pl.ds(off, FFN_CHUNK)]
        ac = cb + p2 * cw[0:1, :] + p1 * cw[1:2, :] + a * cw[2:3, :]
        h = (jax.nn.gelu(ac, approximate=True) * g).astype(BF16)
        acc_ref[...] += jnp.dot(h, wdn_ref[pl.ds(off, FFN_CHUNK), :], preferred_element_type=F32)
        if write_tail is None:
            a_out_ref[:, pl.ds(off, FFN_CHUNK)] = a
        else:
            @pl.when(write_tail)
            def _():
                a_out_ref[0, :, pl.ds(off, FFN_CHUNK)] = a[tm - V7X_SUBLANES:tm, :]
        return carry

    lax.fori_loop(0, FFN_DIM // FFN_CHUNK, chunk, 0)


def _ffn_seq_kernel(x_ref, xh_ref, wup_ref, wdn_ref, cw_ref, cb_ref, g_ref, b_ref, out_ref, tail_ref, acc_ref):
    i = pl.program_id(1)
    x = x_ref[...]
    xb = x.astype(BF16)
    xh = jnp.where(i == 0, 0.0, xh_ref[...]).astype(BF16)
    _ffn_chunks(xb, xh, None, wup_ref, wdn_ref, cw_ref, cb_ref, acc_ref, tail_ref, i == pl.num_programs(1) - 1)
    out_ref[...] = _layer_norm(DEEPNORM_ALPHA * x + acc_ref[...], g_ref[...], b_ref[...])


def _ffn_state_kernel(x_ref, p2_ref, p1_ref, wup_ref, wdn_ref, cw_ref, cb_ref, g_ref, b_ref, out_ref, a_ref, acc_ref):
    x = x_ref[...]
    _ffn_chunks(x.astype(BF16), None, (p2_ref, p1_ref), wup_ref, wdn_ref, cw_ref, cb_ref, acc_ref, a_ref, None)
    out_ref[...] = _layer_norm(DEEPNORM_ALPHA * x + acc_ref[...], g_ref[...], b_ref[...])


def _ffn_seq(x, wup, wdn, cw, cb, g, b, *, nb, tm):
    m = x.shape[0]
    nt = m // nb // tm
    hb = tm // V7X_SUBLANES
    row = pl.BlockSpec((tm, D_MODEL), lambda s, i: (s * nt + i, 0))
    halo = pl.BlockSpec((V7X_SUBLANES, D_MODEL), lambda s, i: (jnp.maximum((s * nt + i) * hb - 1, 0), 0))
    vec_d = pl.BlockSpec((1, D_MODEL), lambda s, i: (0, 0))
    return pl.pallas_call(
        _ffn_seq_kernel,
        grid=(nb, nt),
        in_specs=[row, halo, _resident(wup.shape), _resident(wdn.shape), _resident(cw.shape),
                  _resident((1, FFN_DIM)), vec_d, vec_d],
        out_specs=[row, pl.BlockSpec((1, V7X_SUBLANES, FFN_DIM), lambda s, i: (s, 0, 0))],
        out_shape=[jax.ShapeDtypeStruct((m, D_MODEL), F32),
                   jax.ShapeDtypeStruct((nb, V7X_SUBLANES, FFN_DIM), F32)],
        scratch_shapes=[pltpu.VMEM((tm, D_MODEL), F32)],
        compiler_params=_params("arbitrary", "arbitrary"),
        name="conv_ffn_seq",
    )(x, x, wup, wdn, cw, cb.reshape(1, -1), g.reshape(1, -1), b.reshape(1, -1))


def _ffn_state(x, p2, p1, wup, wdn, cw, cb, g, b):
    m = x.shape[0]
    full = lambda shape: pl.BlockSpec(shape, lambda i: (0,) * len(shape))
    return pl.pallas_call(
        _ffn_state_kernel,
        grid=(1,),
        in_specs=[full((m, D_MODEL)), full((m, FFN_DIM)), full((m, FFN_DIM)), _resident(wup.shape),
                  _resident(wdn.shape), _resident(cw.shape), _resident((1, FFN_DIM)),
                  full((1, D_MODEL)), full((1, D_MODEL))],
        out_specs=[full((m, D_MODEL)), full((m, FFN_DIM))],
        out_shape=[jax.ShapeDtypeStruct((m, D_MODEL), F32), jax.ShapeDtypeStruct((m, FFN_DIM), F32)],
        scratch_shapes=[pltpu.VMEM((m, D_MODEL), F32)],
        compiler_params=_params("arbitrary"),
        name="conv_ffn_state",
    )(x, p2, p1, wup, wdn, cw, cb.reshape(1, -1), g.reshape(1, -1), b.reshape(1, -1))


def _softmax_step(s, v, m_ref, l_ref, acc_ref):
    nl = s.shape[1] // V7X_LANES
    tiles = [s[:, j * V7X_LANES:(j + 1) * V7X_LANES] for j in range(nl)]
    m_prev = m_ref[...]
    m_new = jnp.maximum(m_prev, jnp.max(functools.reduce(jnp.maximum, tiles), axis=1, keepdims=True))
    alpha = jnp.exp(m_prev - m_new)
    ps = [jnp.exp(t - m_new) for t in tiles]
    l_ref[...] = alpha * l_ref[...] + functools.reduce(jnp.add, ps)
    p = jnp.concatenate(ps, axis=1).astype(BF16)
    acc_ref[...] = alpha * acc_ref[...] + jnp.dot(p, v, preferred_element_type=F32)
    m_ref[...] = m_new


def _diff_lambda(lam_ref, lam_init):
    lp = lam_ref[...]
    s1 = jnp.sum(lp[0:1, :] * lp[1:2, :], axis=1, keepdims=True)
    s2 = jnp.sum(lp[2:3, :] * lp[3:4, :], axis=1, keepdims=True)
    return jnp.exp(s1) - jnp.exp(s2) + lam_init


def _diff_finish(o, subln, lam_init):
    ms = jnp.mean(o * o, axis=-1, keepdims=True)
    return o * lax.rsqrt(ms + NORM_EPS) * subln * (1.0 - lam_init)


def _causal_keep(qi, j, tq, tk, reps):
    qpos = qi * tq + lax.broadcasted_iota(jnp.int32, (tq, tk), 0)
    kpos = j * tk + lax.broadcasted_iota(jnp.int32, (tq, tk), 1)
    keep = jnp.where(kpos <= qpos, 1.0, 0.0)
    return jnp.concatenate([keep] * reps, axis=0) > 0.5


def _diff_attn_kernel(lam_ref, subln_ref, q_ref, k_ref, v_ref, o_ref, m_s, l_s, a_s, *, tq, tk, lam_init):
    qi = pl.program_id(2)
    q = q_ref[0]
    lane = lax.broadcasted_iota(jnp.int32, q.shape, 1)
    zero = jnp.zeros_like(q)
    q12 = jnp.concatenate([jnp.where(lane < HEAD_DIM, q, zero), jnp.where(lane >= HEAD_DIM, q, zero)], axis=0)
    m_s[...] = jnp.full(m_s.shape, NEG_BIG, F32)
    l_s[...] = jnp.zeros(l_s.shape, F32)
    a_s[...] = jnp.zeros(a_s.shape, F32)
    nt_dims = (((1,), (1,)), ((), ()))

    def step(j, masked):
        off = pl.multiple_of(j * tk, tk)
        s = lax.dot_general(q12, k_ref[0, pl.ds(off, tk), :], nt_dims, preferred_element_type=F32)
        if masked:
            s = jnp.where(_causal_keep(qi, j, tq, tk, 2), s, NEG_BIG)
        _softmax_step(s, v_ref[0, pl.ds(off, tk), :], m_s, l_s, a_s)

    n_full = (qi * tq) // tk
    n_all = ((qi + 1) * tq + tk - 1) // tk
    lax.fori_loop(0, n_full, lambda j, c: (step(j, False), c)[1], 0)
    lax.fori_loop(n_full, n_all, lambda j, c: (step(j, True), c)[1], 0)

    lam = _diff_lambda(lam_ref, lam_init)
    on = a_s[...] / jnp.sum(l_s[...], axis=1, keepdims=True)
    o = on[0:tq, :] - lam * on[tq:2 * tq, :]
    o_ref[0] = _diff_finish(o, subln_ref[...], lam_init).astype(BF16)


def _diff_attn(q, k, v, lam_p, subln, lam_init, *, tq, tk):
    nb, t, _ = q.shape
    qspec = pl.BlockSpec((1, tq, V7X_LANES), lambda b, h, i: (b, i, h))
    kvspec = pl.BlockSpec((1, t, V7X_LANES), lambda b, h, i: (b, 0, h))
    small = lambda shape: pl.BlockSpec(shape, lambda b, h, i: (0,) * len(shape))
    stat = pltpu.VMEM((2 * tq, V7X_LANES), F32)
    return pl.pallas_call(
        functools.partial(_diff_attn_kernel, tq=tq, tk=tk, lam_init=lam_init),
        grid=(nb, DIFF_HEADS, t // tq),
        in_specs=[small((4, HEAD_DIM)), small((1, 2 * HEAD_DIM)), qspec, kvspec, kvspec],
        out_specs=qspec,
        out_shape=jax.ShapeDtypeStruct((nb, t, D_MODEL), BF16),
        scratch_shapes=[stat, stat, stat],
        compiler_params=_params("arbitrary", "arbitrary", "arbitrary"),
        name="diff_attn",
    )(lam_p, subln.reshape(1, -1), q, k, v)


def _beats(row, g, tie_wins):
    return jnp.where(tie_wins, jnp.where(row >= g, 1.0, 0.0), jnp.where(row > g, 1.0, 0.0))


def _moba_select_bias(g_t, own, n_blocks):
    blk = lax.broadcasted_iota(jnp.int32, g_t.shape, 0)
    cnt = jnp.zeros(g_t.shape, F32)
    for n in range(n_blocks):
        cnt = cnt + _beats(g_t[n:n + 1, :], g_t, blk > n) * jnp.where(n < own, 1.0, 0.0)
    visible = jnp.where(cnt < MOBA_TOPK, 1.0, 0.0) * jnp.where(blk < own, 1.0, 0.0) + jnp.where(blk == own, 1.0, 0.0)
    return jnp.where(visible > 0.5, 0.0, NEG_BIG)


def _moba_attn_kernel(q_ref, k_ref, v_ref, mean_ref, o_ref, m_s, l_s, a_s, *, tq, tk, n_blocks):
    qi = pl.program_id(2)
    q = q_ref[0]
    lane = lax.broadcasted_iota(jnp.int32, q.shape, 1)
    zero = jnp.zeros_like(q)
    nt_dims = (((1,), (1,)), ((), ()))
    means = mean_ref[0].astype(BF16)
    own = (qi * tq + lax.broadcasted_iota(jnp.int32, (1, tq), 1)) // MOBA_BLOCK
    eye = (lax.broadcasted_iota(jnp.int32, (tq, tq), 0) == lax.broadcasted_iota(jnp.int32, (tq, tq), 1))
    eye = jnp.where(eye, 1.0, 0.0).astype(BF16)
    pad = jnp.zeros((V7X_LANES - n_blocks, tq), BF16)
    qs, sels = [], []
    for qq in (jnp.where(lane < HEAD_DIM, q, zero), jnp.where(lane >= HEAD_DIM, q, zero)):
        g_t = lax.dot_general(means, qq, nt_dims, preferred_element_type=F32)
        bias_t = jnp.concatenate([_moba_select_bias(g_t, own, n_blocks).astype(BF16), pad], axis=0)
        sels.append(lax.dot_general(eye, bias_t, nt_dims, preferred_element_type=F32).astype(BF16))
        qs.append(qq)
    q_aug = jnp.concatenate([jnp.concatenate(qs, axis=0), jnp.concatenate(sels, axis=0)], axis=1)
    m_s[...] = jnp.full(m_s.shape, NEG_BIG, F32)
    l_s[...] = jnp.zeros(l_s.shape, F32)
    a_s[...] = jnp.zeros(a_s.shape, F32)
    blocks_per_step = tk // MOBA_BLOCK

    def step(j, masked):
        off = pl.multiple_of(j * tk, tk)
        kblk = j * blocks_per_step + lax.broadcasted_iota(jnp.int32, (tk, V7X_LANES), 0) // MOBA_BLOCK
        onehot = jnp.where(lax.broadcasted_iota(jnp.int32, (tk, V7X_LANES), 1) == kblk, 1.0, 0.0).astype(BF16)
        k_aug = jnp.concatenate([k_ref[0, pl.ds(off, tk), :], onehot], axis=1)
        s = lax.dot_general(q_aug, k_aug, nt_dims, preferred_element_type=F32)
        if masked:
            s = jnp.where(_causal_keep(qi, j, tq, tk, 2), s, NEG_BIG)
        _softmax_step(s, v_ref[0, pl.ds(off, tk), :], m_s, l_s, a_s)

    n_full = (qi * tq) // tk
    n_all = ((qi + 1) * tq + tk - 1) // tk
    lax.fori_loop(0, n_full, lambda j, c: (step(j, False), c)[1], 0)
    lax.fori_loop(n_full, n_all, lambda j, c: (step(j, True), c)[1], 0)
    on = a_s[...] / jnp.sum(l_s[...], axis=1, keepdims=True)
    o_ref[0] = jnp.where(lane < HEAD_DIM, on[0:tq, :], on[tq:2 * tq, :]).astype(BF16)


def _moba_attn(q, k, v, kmeans, *, tq, tk):
    nb, t, _ = q.shape
    n_blocks = t // MOBA_BLOCK
    qspec = pl.BlockSpec((1, tq, V7X_LANES), lambda b, h, i: (b, i, h))
    kvspec = pl.BlockSpec((1, t, V7X_LANES), lambda b, h, i: (b, 0, h))
    mspec = pl.BlockSpec((1, n_blocks, V7X_LANES), lambda b, h, i: (b, 0, h))
    stat = pltpu.VMEM((2 * tq, V7X_LANES), F32)
    return pl.pallas_call(
        functools.partial(_moba_attn_kernel, tq=tq, tk=tk, n_blocks=n_blocks),
        grid=(nb, MOBA_HEADS // 2, t // tq),
        in_specs=[qspec, kvspec, kvspec, mspec],
        out_specs=qspec,
        out_shape=jax.ShapeDtypeStruct((nb, t, D_MODEL), BF16),
        scratch_shapes=[stat, stat, stat],
        compiler_params=_params("arbitrary", "arbitrary", "arbitrary"),
        name="moba_attn",
    )(q, k, v, kmeans)


def _softplus(z):
    return jnp.maximum(z, 0.0) + jnp.log1p(jnp.exp(-jnp.abs(z)))


def _rg_gates(uc, wa_ref, ba_ref, wi_ref, bi_ref, lam_ref, a_ref, b_ref):
    sp = _softplus(-lam_ref[...])
    for n in range(RG_BLOCKS):
        sl = slice(n * RG_BW, (n + 1) * RG_BW)
        ub = uc[:, sl]
        ubb = ub.astype(BF16)
        r = jax.nn.sigmoid(jnp.dot(ubb, wa_ref[n], preferred_element_type=F32) + ba_ref[:, sl])
        ig = jax.nn.sigmoid(jnp.dot(ubb, wi_ref[n], preferred_element_type=F32) + bi_ref[:, sl])
        log_a = -RG_C * r * sp[:, sl]
        a = jnp.exp(log_a)
        a_ref[:, sl] = a
        b_ref[:, sl] = jnp.sqrt(1.0 - jnp.exp(2.0 * log_a)) * (ig * ub)


def _rg_seq_kernel(x_ref, xh_ref, win_ref, cw_ref, cb_ref, wa_ref, ba_ref, wi_ref, bi_ref, lam_ref,
                   y_ref, hl_ref, tail_ref, gate_s, a_s, b_s, h_s, carry_s):
    i = pl.program_id(1)
    tm = x_ref.shape[0]
    w = RG_WIDTH
    xb = x_ref[...].astype(BF16)
    xh = jnp.where(i == 0, 0.0, xh_ref[...]).astype(BF16)
    gate_s[...] = jnp.dot(xb, win_ref[:, 0:w], preferred_element_type=F32)
    u = jnp.dot(xb, win_ref[:, w:2 * w], preferred_element_type=F32)
    uh = jnp.dot(xh, win_ref[:, w:2 * w], preferred_element_type=F32)
    cw = cw_ref[...]
    uc = cb_ref[...] + u * cw[3:4, :]
    for s in range(1, RG_CONV):
        uc = uc + _shift_rows(u, uh, s) * cw[3 - s:4 - s, :]
    _rg_gates(uc, wa_ref, ba_ref, wi_ref, bi_ref, lam_ref, a_s, b_s)

    @pl.when(i == 0)
    def _():
        carry_s[...] = jnp.zeros_like(carry_s)

    row = lax.broadcasted_iota(jnp.int32, (V7X_SUBLANES, w), 0)

    def group(gi, c):
        off = pl.multiple_of(gi * V7X_SUBLANES, V7X_SUBLANES)
        a8 = a_s[pl.ds(off, V7X_SUBLANES), :]
        b8 = b_s[pl.ds(off, V7X_SUBLANES), :]
        for s in (1, 2, 4):
            ok = row >= s
            b8 = jnp.where(ok, a8 * pltpu.roll(b8, s, axis=0) + b8, b8)
            a8 = jnp.where(ok, a8 * pltpu.roll(a8, s, axis=0), a8)
        h8 = a8 * carry_s[V7X_SUBLANES - 1:V7X_SUBLANES, :] + b8
        h_s[pl.ds(off, V7X_SUBLANES), :] = h8
        carry_s[...] = h8
        return c

    lax.fori_loop(0, tm // V7X_SUBLANES, group, 0)
    y_ref[...] = (jax.nn.gelu(gate_s[...], approximate=True) * h_s[...]).astype(BF16)

    @pl.when(i == pl.num_programs(1) - 1)
    def _():
        hl_ref[0] = carry_s[...]
        tail_ref[0] = u[tm - V7X_SUBLANES:tm, :]


def _rg_seq(x, win, cw, cb, wa, ba, wi, bi, lam, *, nb, tm):
    m = x.shape[0]
    nt = m // nb // tm
    hb = tm // V7X_SUBLANES
    w = RG_WIDTH
    row = pl.BlockSpec((tm, D_MODEL), lambda s, i: (s * nt + i, 0))
    halo = pl.BlockSpec((V7X_SUBLANES, D_MODEL), lambda s, i: (jnp.maximum((s * nt + i) * hb - 1, 0), 0))
    vec = _resident((1, w))
    per_seq = pl.BlockSpec((1, V7X_SUBLANES, w), lambda s, i: (s, 0, 0))
    big = pltpu.VMEM((tm, w), F32)
    return pl.pallas_call(
        _rg_seq_kernel,
        grid=(nb, nt),
        in_specs=[row, halo, _resident(win.shape), _resident(cw.shape), vec, _resident(wa.shape), vec,
                  _resident(wi.shape), vec, vec],
        out_specs=[pl.BlockSpec((tm, w), lambda s, i: (s * nt + i, 0)), per_seq, per_seq],
        out_shape=[jax.ShapeDtypeStruct((m, w), BF16), jax.ShapeDtypeStruct((nb, V7X_SUBLANES, w), F32),
                   jax.ShapeDtypeStruct((nb, V7X_SUBLANES, w), F32)],
        scratch_shapes=[big, big, big, big, pltpu.VMEM((V7X_SUBLANES, w), F32)],
        compiler_params=_params("arbitrary", "arbitrary"),
        name="rglru_seq",
    )(x, x, win, cw, cb.reshape(1, -1), wa, ba.reshape(1, -1), wi, bi.reshape(1, -1), lam.reshape(1, -1))


def _rg_state_kernel(x_ref, h0_ref, c0_ref, c1_ref, c2_ref, win_ref, cw_ref, cb_ref, wa_ref, ba_ref, wi_ref, bi_ref,
                     lam_ref, y_ref, h_ref, u_ref, a_s, b_s):
    w = RG_WIDTH
    xb = x_ref[...].astype(BF16)
    gate = jnp.dot(xb, win_ref[:, 0:w], preferred_element_type=F32)
    u = jnp.dot(xb, win_ref[:, w:2 * w], preferred_element_type=F32)
    cw = cw_ref[...]
    uc = cb_ref[...] + c0_ref[...] * cw[0:1, :] + c1_ref[...] * cw[1:2, :] + c2_ref[...] * cw[2:3, :] + u * cw[3:4, :]
    _rg_gates(uc, wa_ref, ba_ref, wi_ref, bi_ref, lam_ref, a_s, b_s)
    h = a_s[...] * h0_ref[...] + b_s[...]
    h_ref[...] = h
    u_ref[...] = u
    y_ref[...] = (jax.nn.gelu(gate, approximate=True) * h).astype(BF16)


def _rg_state(x, h0, c0, c1, c2, win, cw, cb, wa, ba, wi, bi, lam):
    m = x.shape[0]
    w = RG_WIDTH
    full = lambda shape: pl.BlockSpec(shape, lambda i: (0,) * len(shape))
    st = full((m, w))
    vec = full((1, w))
    return pl.pallas_call(
        _rg_state_kernel,
        grid=(1,),
        in_specs=[full((m, D_MODEL)), st, st, st, st, _resident(win.shape), full(cw.shape), vec,
                  _resident(wa.shape), vec, _resident(wi.shape), vec, vec],
        out_specs=[st, st, st],
        out_shape=[jax.ShapeDtypeStruct((m, w), BF16), jax.ShapeDtypeStruct((m, w), F32),
                   jax.ShapeDtypeStruct((m, w), F32)],
        scratch_shapes=[pltpu.VMEM((m, w), F32), pltpu.VMEM((m, w), F32)],
        compiler_params=_params("arbitrary"),
        name="rglru_state",
    )(x, h0, c0, c1, c2, win, cw, cb.reshape(1, -1), wa, ba.reshape(1, -1), wi, bi.reshape(1, -1), lam.reshape(1, -1))


def _page_scores(k_refs, q_b):
    return jnp.concatenate([jnp.sum(kr[0] * q_b, axis=1) for kr in k_refs], axis=1)


def _diff_decode_kernel(pt_ref, lam_ref, subln_ref, qb_ref, q_ref, kn_ref, vn_ref, *rest, n_pg, lam_init):
    del pt_ref
    k_refs, v_refs = rest[:n_pg], rest[n_pg:2 * n_pg]
    o_ref, m_s, l_s, a_s = rest[2 * n_pg:]
    step = pl.program_id(1)
    h = DIFF_HEADS

    @pl.when(step == 0)
    def _():
        s_new = jnp.sum(kn_ref[0] * q_ref[0], axis=-1, keepdims=True)
        m_s[...] = jnp.broadcast_to(s_new, m_s.shape)
        l_s[...] = jnp.full(l_s.shape, 1.0 / V7X_LANES, F32)
        a_s[...] = vn_ref[0]

    s = _page_scores(k_refs, qb_ref[0])
    m_prev = m_s[...]
    m_new = jnp.maximum(m_prev, jnp.max(s, axis=1, keepdims=True))
    alpha = jnp.exp(m_prev - m_new)
    p = jnp.exp(s - jnp.concatenate([m_new] * n_pg, axis=1))
    l_s[...] = alpha * l_s[...] + functools.reduce(
        jnp.add, [p[:, g * V7X_LANES:(g + 1) * V7X_LANES] for g in range(n_pg)])
    pb = p.astype(BF16)
    wide = PAGE_SIZE * h
    col = lax.broadcasted_iota(jnp.int32, (PAGE_SIZE, wide), 1)
    spread = jnp.where(col // h == lax.broadcasted_iota(jnp.int32, (PAGE_SIZE, wide), 0), 1.0, 0.0).astype(BF16)
    mine = (lax.broadcasted_iota(jnp.int32, (2 * h, wide), 1) % h) == (lax.broadcasted_iota(jnp.int32, (2 * h, wide), 0) // 2)
    pv = jnp.zeros(a_s.shape, F32)
    for g, vr in enumerate(v_refs):
        p_wide = jnp.dot(pb[:, g * V7X_LANES:(g + 1) * V7X_LANES], spread, preferred_element_type=F32)
        p_wide = jnp.where(mine, p_wide, 0.0).astype(BF16)
        v2 = vr[0].reshape(wide, 2 * HEAD_DIM).astype(BF16)
        pv = pv + jnp.dot(p_wide, v2, preferred_element_type=F32)
    a_s[...] = alpha * a_s[...] + pv
    m_s[...] = m_new

    @pl.when(step == pl.num_programs(1) - 1)
    def _():
        lam = _diff_lambda(lam_ref, lam_init)
        a_s[...] = a_s[...] / jnp.sum(l_s[...], axis=1, keepdims=True)
        o = a_s[pl.ds(0, h, stride=2), :] - lam * a_s[pl.ds(1, h, stride=2), :]
        o_ref[0] = _diff_finish(o, subln_ref[...], lam_init).astype(BF16)


def _decode_specs(page_table, n_pg, kshape, vshape):
    per_seq = lambda shape: pl.BlockSpec((1,) + shape, lambda b, s, pt: (b,) + (0,) * len(shape))

    def page(shape, g):
        return pl.BlockSpec((1,) + shape, lambda b, s, pt: (pt[b, s * n_pg + g],) + (0,) * len(shape))

    return per_seq, [page(kshape, g) for g in range(n_pg)] + [page(vshape, g) for g in range(n_pg)]


def _diff_decode(q, k_new, v_new, cache_k, cache_v, page_table, lam_p, subln, lam_init):
    nb, n_pages = page_table.shape
    n_pg = math.gcd(DECODE_PAGES, n_pages)
    h, c = DIFF_HEADS, 2 * DIFF_HEADS
    k_t = jnp.transpose(cache_k, (0, 2, 3, 1))
    q_b = jnp.broadcast_to(q[..., None], (nb, c, HEAD_DIM, PAGE_SIZE))
    v_rows = jnp.repeat(v_new, 2, axis=1)
    per_seq, pages = _decode_specs(page_table, n_pg, (c, HEAD_DIM, PAGE_SIZE), (PAGE_SIZE, h, 2 * HEAD_DIM))
    small = lambda shape: pl.BlockSpec(shape, lambda b, s, pt: (0,) * len(shape))
    stat = pltpu.VMEM((c, V7X_LANES), F32)
    return pl.pallas_call(
        functools.partial(_diff_decode_kernel, n_pg=n_pg, lam_init=lam_init),
        grid_spec=pltpu.PrefetchScalarGridSpec(
            num_scalar_prefetch=1,
            grid=(nb, n_pages // n_pg),
            in_specs=[small((4, HEAD_DIM)), small((1, 2 * HEAD_DIM)), per_seq((c, HEAD_DIM, PAGE_SIZE)),
                      per_seq((c, HEAD_DIM)), per_seq((c, HEAD_DIM)), per_seq((c, 2 * HEAD_DIM))] + pages,
            out_specs=per_seq((h, 2 * HEAD_DIM)),
            scratch_shapes=[stat, stat, stat],
        ),
        out_shape=jax.ShapeDtypeStruct((nb, h, 2 * HEAD_DIM), BF16),
        compiler_params=_params("arbitrary", "arbitrary"),
        name="diff_decode",
    )(page_table, lam_p, subln.reshape(1, -1), q_b, q, k_new, v_rows, *([k_t] * n_pg), *([cache_v] * n_pg))


def _moba_decode_kernel(pt_ref, qb_ref, q_ref, kn_ref, vn_ref, *rest, n_pg, n_pages):
    del pt_ref
    k_refs, v_refs = rest[:n_pg], rest[n_pg:2 * n_pg]
    o_ref, s_s, m_s, l_s, r_s = rest[2 * n_pg:]
    step = pl.program_id(1)
    c = MOBA_HEADS
    nt_dims = (((1,), (1,)), ((), ()))
    s = _page_scores(k_refs, qb_ref[0])
    for g in range(n_pg):
        sg = s[:, g * V7X_LANES:(g + 1) * V7X_LANES]
        mg = jnp.broadcast_to(jnp.max(sg, axis=1, keepdims=True), sg.shape)
        pg = jnp.exp(sg - mg)
        page = step * n_pg + g
        s_s[page] = sg
        m_s[page] = mg
        l_s[page] = pg
        v2 = v_refs[g][0].reshape(c * HEAD_DIM, PAGE_SIZE).astype(BF16)
        r_s[page] = lax.dot_general(pg.astype(BF16), v2, nt_dims, preferred_element_type=F32)

    @pl.when(step == pl.num_programs(1) - 1)
    def _():
        pages_per_block = MOBA_BLOCK // PAGE_SIZE
        nblk = n_pages // pages_per_block
        wide = (n_pages, c, V7X_LANES)
        gsum = jnp.sum(s_s[...], axis=2, keepdims=True).reshape(nblk, pages_per_block, c, 1)
        gate = jnp.broadcast_to(jnp.sum(gsum, axis=1), (nblk, c, V7X_LANES))
        cnt = jnp.zeros(gate.shape, F32)
        blk = lax.broadcasted_iota(jnp.int32, gate.shape, 0)
        for n in range(nblk):
            cnt = cnt + _beats(gate[n:n + 1], gate, blk > n)
        selw = jnp.where(cnt < MOBA_TOPK, 1.0, 0.0)
        selw = jnp.broadcast_to(selw[:, None], (nblk, pages_per_block, c, V7X_LANES)).reshape(wide)
        sel = selw > 0.5
        s_new = jnp.broadcast_to(jnp.sum(kn_ref[0] * q_ref[0], axis=-1, keepdims=True), (c, V7X_LANES))
        mp = jnp.where(sel, m_s[...], NEG_BIG)
        m_all = jnp.maximum(jnp.max(mp, axis=0), s_new)
        w = jnp.where(sel, jnp.exp(mp - m_all[None]), 0.0)
        w_new = jnp.exp(s_new - m_all)
        lsum = jnp.sum(l_s[...], axis=2, keepdims=True)
        den = jnp.sum(w * lsum, axis=0) + w_new
        reps = c * HEAD_DIM // V7X_LANES
        num = jnp.sum(jnp.concatenate([w] * reps, axis=2) * r_s[...], axis=0)
        num = num + jnp.concatenate([w_new] * reps, axis=1) * vn_ref[0]
        row = lax.broadcasted_iota(jnp.int32, (c, HEAD_DIM), 0)
        out = jnp.zeros((c, HEAD_DIM), F32)
        for cc in range(c):
            out = jnp.where(row == cc, num[:, cc * HEAD_DIM:(cc + 1) * HEAD_DIM], out)
        o_ref[0] = (out / den[:, 0:HEAD_DIM]).astype(BF16)


def _moba_decode(q, k_new, v_new, cache_k, cache_v, page_table):
    nb, n_pages = page_table.shape
    n_pg = math.gcd(DECODE_PAGES, n_pages)
    c = MOBA_HEADS
    k_t = jnp.transpose(cache_k, (0, 2, 3, 1))
    v_t = jnp.transpose(cache_v, (0, 2, 3, 1))
    q_b = jnp.broadcast_to(q[..., None], (nb, c, HEAD_DIM, PAGE_SIZE))
    v_wide = jnp.tile(v_new, (1, 1, c))
    pshape = (c, HEAD_DIM, PAGE_SIZE)
    per_seq, pages = _decode_specs(page_table, n_pg, pshape, pshape)
    part = pltpu.VMEM((n_pages, c, V7X_LANES), F32)
    return pl.pallas_call(
        functools.partial(_moba_decode_kernel, n_pg=n_pg, n_pages=n_pages),
        grid_spec=pltpu.PrefetchScalarGridSpec(
            num_scalar_prefetch=1,
            grid=(nb, n_pages // n_pg),
            in_specs=[per_seq(pshape), per_seq((c, HEAD_DIM)), per_seq((c, HEAD_DIM)),
                      per_seq((c, c * HEAD_DIM))] + pages,
            out_specs=per_seq((c, HEAD_DIM)),
            scratch_shapes=[part, part, part, pltpu.VMEM((n_pages, c, c * HEAD_DIM), F32)],
        ),
        out_shape=jax.ShapeDtypeStruct((nb, c, HEAD_DIM), BF16),
        compiler_params=_params("arbitrary", "arbitrary"),
        name="moba_decode",
    )(page_table, q_b, q, k_new, v_wide, *([k_t] * n_pg), *([v_t] * n_pg))


def kernel(x_prompt, x_sample, cache_k_l0, cache_v_l0, cache_k_l1, cache_v_l1, cache_k_l3, cache_v_l3, state_rg_h_l2, state_rg_conv_l2, state_ffn_conv, page_table, l0_w_qkv, l0_lambda, l0_subln, l0_w_o, l1_w_qkv, l1_w_o, l2_w_in, l2_conv_w, l2_conv_b, l2_w_a, l2_b_a, l2_w_i, l2_b_i, l2_lambda, l2_w_o, l3_w_qkv, l3_lambda, l3_subln, l3_w_o, ffn_w_up, ffn_conv_w, ffn_conv_b, ffn_w_down, ln_mix_g, ln_mix_b, ln_ffn_g, ln_ffn_b):
    nb, t, d = x_prompt.shape
    ns = x_sample.shape[0]
    past_len = page_table.shape[1] * PAGE_SIZE
    tm = min(ROW_TILE, t)
    tab_p = _rope_tables(jnp.arange(t, dtype=jnp.int32))
    tab_s = _rope_tables(jnp.full((ns,), past_len, dtype=jnp.int32))
    xp = x_prompt.reshape(nb * t, d)
    xs = x_sample.reshape(ns, d)
    outs_p, outs_s, ffn_p, ffn_s = {}, {}, [], []

    def attn_layer(i, w_qkv, w_o, cache_k, cache_v, lam_p=None, subln=None):
        nonlocal xp, xs
        wq = w_qkv.astype(BF16)
        wo = w_o.astype(BF16)
        diff = lam_p is not None
        res_p = _qkv_rope(xp, wq, tab_p, tm=tm, rows_per_seq=t, with_means=not diff)
        q_p, k_p, v_p, kb_p, vb_p = res_p[:5]
        q_s, k_s, v_s, _, _ = _qkv_rope(xs, wq, tab_s, tm=ns, rows_per_seq=ns, with_means=False)
        shp = lambda a: a.reshape(nb, t, d)
        q_s3 = q_s.astype(F32).reshape(ns, 2 * DIFF_HEADS, HEAD_DIM)
        k_s3 = k_s.reshape(ns, 2 * DIFF_HEADS, HEAD_DIM)
        if diff:
            lam_init = 0.8 - 0.6 * math.exp(-0.3 * i)
            o_p = _diff_attn(shp(q_p), shp(kb_p), shp(vb_p), lam_p, subln, lam_init, tq=min(ATT_TQ, t), tk=min(ATT_TK, t))
            v_s3 = v_s.reshape(ns, DIFF_HEADS, 2 * HEAD_DIM)
            o_s = _diff_decode(q_s3, k_s3, v_s3, cache_k, cache_v, page_table, lam_p, subln, lam_init)
            kv_shape = ((2 * DIFF_HEADS, HEAD_DIM), (DIFF_HEADS, 2 * HEAD_DIM))
        else:
            kmeans = res_p[5].reshape(nb, t // MOBA_BLOCK, d)
            o_p = _moba_attn(shp(q_p), shp(kb_p), shp(vb_p), kmeans, tq=min(ATT_TQ, t), tk=min(ATT_TK, t))
            v_s3 = v_s.reshape(ns, MOBA_HEADS, HEAD_DIM)
            o_s = _moba_decode(q_s3, k_s3, v_s3, cache_k, cache_v, page_table)
            kv_shape = ((MOBA_HEADS, HEAD_DIM), (MOBA_HEADS, HEAD_DIM))
        outs_p[i] = (k_p.reshape(nb, t, *kv_shape[0]), v_p.reshape(nb, t, *kv_shape[1]))
        outs_s[i] = (k_s.reshape(ns, 1, *kv_shape[0]), v_s.reshape(ns, 1, *kv_shape[1]))
        xp = _proj_ln(o_p.reshape(nb * t, d), xp, wo, ln_mix_g[i], ln_mix_b[i], tm=tm)
        xs = _proj_ln(o_s.reshape(ns, d), xs, wo, ln_mix_g[i], ln_mix_b[i], tm=ns)

    def rg_layer(i):
        nonlocal xp, xs
        win = l2_w_in.astype(BF16)
        wa = l2_w_a.astype(BF16)
        wi = l2_w_i.astype(BF16)
        wo = l2_w_o.astype(BF16)
        y_p, hl_p, tail_p = _rg_seq(xp, win, l2_conv_w, l2_conv_b, wa, l2_b_a, wi, l2_b_i, l2_lambda, nb=nb, tm=tm)
        cs = state_rg_conv_l2
        y_s, h_s, u_s = _rg_state(xs, state_rg_h_l2, cs[:, 0], cs[:, 1], cs[:, 2], win, l2_conv_w, l2_conv_b,
                                  wa, l2_b_a, wi, l2_b_i, l2_lambda)
        outs_p[i] = (hl_p[:, V7X_SUBLANES - 1], tail_p[:, V7X_SUBLANES - (RG_CONV - 1):])
        outs_s[i] = (h_s, jnp.concatenate([cs[:, 1:], u_s[:, None]], axis=1))
        xp = _proj_ln(y_p, xp, wo, ln_mix_g[i], ln_mix_b[i], tm=tm)
        xs = _proj_ln(y_s, xs, wo, ln_mix_g[i], ln_mix_b[i], tm=ns)

    def ffn_layer(i):
        nonlocal xp, xs
        wup = ffn_w_up[i].astype(BF16)
        wdn = ffn_w_down[i].astype(BF16)
        xp, tail = _ffn_seq(xp, wup, wdn, ffn_conv_w[i], ffn_conv_b[i], ln_ffn_g[i], ln_ffn_b[i], nb=nb,
                            tm=min(FFN_ROW_TILE, t))
        st = state_ffn_conv[i]
        xs, a_s = _ffn_state(xs, st[:, 0], st[:, 1], wup, wdn, ffn_conv_w[i], ffn_conv_b[i], ln_ffn_g[i], ln_ffn_b[i])
        ffn_p.append(tail[:, V7X_SUBLANES - (FFN_CONV - 1):])
        ffn_s.append(jnp.concatenate([st[:, 1:], a_s[:, None]], axis=1))

    attn_layer(0, l0_w_qkv, l0_w_o, cache_k_l0, cache_v_l0, l0_lambda, l0_subln)
    ffn_layer(0)
    attn_layer(1, l1_w_qkv, l1_w_o, cache_k_l1, cache_v_l1)
    ffn_layer(1)
    rg_layer(2)
    ffn_layer(2)
    attn_layer(3, l3_w_qkv, l3_w_o, cache_k_l3, cache_v_l3, l3_lambda, l3_subln)
    ffn_layer(3)

    return (xp.reshape(nb, t, d), xs.reshape(ns, 1, d),
            outs_p[0][0], outs_p[0][1], outs_p[1][0], outs_p[1][1], outs_p[2][0], outs_p[2][1],
            outs_p[3][0], outs_p[3][1], jnp.stack(ffn_p),
            outs_s[0][0], outs_s[0][1], outs_s[1][0], outs_s[1][1], outs_s[2][0], outs_s[2][1],
            outs_s[3][0], outs_s[3][1], jnp.stack(ffn_s))
```

```python
import functools
import math

import jax
import jax.numpy as jnp
from jax import lax
from jax.experimental import pallas as pl
from jax.experimental.pallas import tpu as pltpu

F32 = jnp.float32
BF16 = jnp.bfloat16

D_MODEL = 1024
DEPTH = 4
PAGE_SIZE = 128
HEAD_DIM = 64
DIFF_HEADS = D_MODEL // (2 * HEAD_DIM)
MOBA_HEADS = D_MODEL // HEAD_DIM
MOBA_BLOCK = 256
MOBA_TOPK = 3
RG_WIDTH = 3 * D_MODEL // 2
RG_BLOCKS = 12
RG_BW = RG_WIDTH // RG_BLOCKS
RG_CONV = 4
RG_C = 8.0
FFN_DIM = 2816
FFN_CONV = 3
ROPE_THETA = 10000.0
NORM_EPS = 1e-5
DEEPNORM_ALPHA = (2.0 * DEPTH) ** 0.25

V7X_LANES = 128
V7X_SUBLANES = 8
V7X_VMEM_BYTES = 64 * 1024 * 1024
VMEM_LIMIT = V7X_VMEM_BYTES * 7 // 8

NEG_BIG = -1e30

ROW_TILE = 512
FFN_ROW_TILE = 1024
FFN_CHUNK = 512
ATT_TQ = 512
ATT_TK = 512
DECODE_PAGES = 8


def _params(*sem):
    return pltpu.CompilerParams(dimension_semantics=sem, vmem_limit_bytes=VMEM_LIMIT)


def _resident(shape):
    nd = len(shape)
    return pl.BlockSpec(shape, lambda *_: (0,) * nd, pipeline_mode=pl.Buffered(1))


def _layer_norm(z, g, b):
    mu = jnp.mean(z, axis=-1, keepdims=True)
    zc = z - mu
    var = jnp.mean(zc * zc, axis=-1, keepdims=True)
    return zc * lax.rsqrt(var + NORM_EPS) * g + b


def _shift_rows(a, halo, s):
    row = lax.broadcasted_iota(jnp.int32, a.shape, 0)
    out = pltpu.roll(a, s, axis=0)
    hn = halo.shape[0]
    for r in range(s):
        out = jnp.where(row == r, halo[hn - s + r:hn - s + r + 1, :], out)
    return out


def _rope_cols(t, cos, sa, sb):
    return t * cos + pltpu.roll(t, HEAD_DIM // 2, axis=1) * sa + pltpu.roll(t, V7X_LANES - HEAD_DIM // 2, axis=1) * sb


def _store_heads(out_ref, cols, x, transposed):
    if transposed:
        pair = V7X_LANES // HEAD_DIM
        first = cols.start // HEAD_DIM
        out_ref[0, first:first + pair, :, :] = x.T.reshape(pair, HEAD_DIM, x.shape[0])
    else:
        out_ref[:, cols] = x


def _qkv_rope_kernel(x_ref, w_ref, cos_ref, sa_ref, sb_ref, q_ref, kb_ref, vb_ref, k_ref, v_ref, *rest,
                     q_scale, n_mean_blocks, k_t, v_t):
    xb = x_ref[...].astype(BF16)
    cos, sa, sb = cos_ref[...], sa_ref[...], sb_ref[...]
    d = D_MODEL
    q = jnp.dot(xb, w_ref[:, 0:d], preferred_element_type=F32)
    k = jnp.dot(xb, w_ref[:, d:2 * d], preferred_element_type=F32)
    v = jnp.dot(xb, w_ref[:, 2 * d:3 * d], preferred_element_type=F32)
    vb_ref[...] = v.astype(BF16)
    sums = [[] for _ in range(n_mean_blocks)]
    for c in range(d // V7X_LANES):
        sl = slice(c * V7X_LANES, (c + 1) * V7X_LANES)
        qc = _rope_cols(q[:, sl], cos, sa, sb)
        kc = _rope_cols(k[:, sl], cos, sa, sb)
        q_ref[:, sl] = (qc * q_scale).astype(BF16)
        kb_ref[:, sl] = kc.astype(BF16)
        _store_heads(k_ref, sl, kc, k_t)
        _store_heads(v_ref, sl, v[:, sl], v_t)
        for r in range(n_mean_blocks):
            sums[r].append(jnp.mean(kc[r * MOBA_BLOCK:(r + 1) * MOBA_BLOCK, :], axis=0, keepdims=True))
    for r in range(n_mean_blocks):
        rest[0][r] = jnp.concatenate(sums[r], axis=1)


def _rope_tables(pos):
    half = HEAD_DIM // 2
    inv_freq = ROPE_THETA ** (-jnp.arange(half, dtype=F32) / half)
    ang = pos.astype(F32)[:, None] * inv_freq[None, :]
    cos = jnp.cos(ang)
    sin = jnp.sin(ang)
    zero = jnp.zeros_like(sin)
    cos128 = jnp.tile(cos, (1, 4))
    sa128 = jnp.tile(jnp.concatenate([zero, sin], axis=1), (1, 2))
    sb128 = jnp.tile(jnp.concatenate([-sin, zero], axis=1), (1, 2))
    return cos128, sa128, sb128


def _qkv_rope(x, w, tables, *, tm, rows_per_seq, with_means, k_t=False, v_t=False):
    m = x.shape[0]
    nt = rows_per_seq // tm
    n_mean = tm // MOBA_BLOCK if with_means else 0
    row = pl.BlockSpec((tm, D_MODEL), lambda i: (i, 0))
    tab = pl.BlockSpec((tm, V7X_LANES), lambda i: (i % nt, 0))
    heads = D_MODEL // HEAD_DIM
    row_f32 = jax.ShapeDtypeStruct((m, D_MODEL), F32)
    t_f32 = jax.ShapeDtypeStruct((m // rows_per_seq, heads, HEAD_DIM, rows_per_seq), F32)
    t_spec = pl.BlockSpec((1, heads, HEAD_DIM, tm), lambda i: (i // nt, 0, 0, i % nt))
    bf = jax.ShapeDtypeStruct((m, D_MODEL), BF16)
    out_shape = [bf, bf, bf, t_f32 if k_t else row_f32, t_f32 if v_t else row_f32]
    out_specs = [row, row, row, t_spec if k_t else row, t_spec if v_t else row]
    if with_means:
        out_shape.append(jax.ShapeDtypeStruct((m // MOBA_BLOCK, 1, D_MODEL), F32))
        out_specs.append(pl.BlockSpec((n_mean, 1, D_MODEL), lambda i: (i, 0, 0)))
    return pl.pallas_call(
        functools.partial(_qkv_rope_kernel, q_scale=HEAD_DIM ** -0.5, n_mean_blocks=n_mean, k_t=k_t, v_t=v_t),
        grid=(m // tm,),
        in_specs=[row, _resident(w.shape), tab, tab, tab],
        out_specs=out_specs,
        out_shape=out_shape,
        compiler_params=_params("arbitrary"),
        name="qkv_rope",
    )(x, w, *tables)


def _proj_ln_kernel(o_ref, x_ref, w_ref, g_ref, b_ref, out_ref):
    y = jnp.dot(o_ref[...], w_ref[...], preferred_element_type=F32)
    out_ref[...] = _layer_norm(DEEPNORM_ALPHA * x_ref[...] + y, g_ref[...], b_ref[...])


def _proj_ln(o, x, w, g, b, *, tm):
    m, kd = o.shape
    row_o = pl.BlockSpec((tm, kd), lambda i: (i, 0))
    row_x = pl.BlockSpec((tm, D_MODEL), lambda i: (i, 0))
    vec = pl.BlockSpec((1, D_MODEL), lambda i: (0, 0))
    return pl.pallas_call(
        _proj_ln_kernel,
        grid=(m // tm,),
        in_specs=[row_o, row_x, _resident(w.shape), vec, vec],
        out_specs=row_x,
        out_shape=jax.ShapeDtypeStruct((m, D_MODEL), F32),
        compiler_params=_params("arbitrary"),
        name="proj_ln",
    )(o, x, w, g.reshape(1, -1), b.reshape(1, -1))


def _ffn_chunks(xb, xh, p_refs, wup_ref, wdn_ref, cw_ref, cb_ref, acc_ref, a_out_ref, write_tail):
    tm = xb.shape[0]

    def chunk(off, width):
        wa = wup_ref[:, off:off + width]
        wg = wup_ref[:, FFN_DIM + off:FFN_DIM + off + width]
        a = jnp.dot(xb, wa, preferred_element_type=F32)
        g = jnp.dot(xb, wg, preferred_element_type=F32)
        if xh is not None:
            ah = jnp.dot(xh, wa, preferred_element_type=F32)
            p1 = _shift_rows(a, ah, 1)
            p2 = _shift_rows(a, ah, 2)
        else:
            p2 = p_refs[0][:, off:off + width]
            p1 = p_refs[1][:, off:off + width]
        cw = cw_ref[:, off:off + width]
        cb = cb_ref[:, off:off + width]
        ac = cb + p2 * cw[0:1, :] + p1 * cw[1:2, :] + a * cw[2:3, :]
        acc_ref[:, off:off + width] = (jax.nn.gelu(ac, approximate=True) * g).astype(BF16)
        if write_tail is None:
            a_out_ref[:, off:off + width] = a
        else:
            @pl.when(write_tail)
            def _():
                a_out_ref[0, :, off:off + width] = a[tm - V7X_SUBLANES:tm, :]

    for off in range(0, FFN_DIM, FFN_CHUNK):
        chunk(off, min(FFN_CHUNK, FFN_DIM - off))
    return jnp.dot(acc_ref[...], wdn_ref[...], preferred_element_type=F32)


def _ffn_seq_kernel(x_ref, xh_ref, wup_ref, wdn_ref, cw_ref, cb_ref, g_ref, b_ref, out_ref, tail_ref, acc_ref):
    i = pl.program_id(1)
    x = x_ref[...]
    xb = x.astype(BF16)
    xh = jnp.where(i == 0, 0.0, xh_ref[...]).astype(BF16)
    y = _ffn_chunks(xb, xh, None, wup_ref, wdn_ref, cw_ref, cb_ref, acc_ref, tail_ref, i == pl.num_programs(1) - 1)
    out_ref[...] = _layer_norm(DEEPNORM_ALPHA * x + y, g_ref[...], b_ref[...])


def _ffn_state_kernel(x_ref, p2_ref, p1_ref, wup_ref, wdn_ref, cw_ref, cb_ref, g_ref, b_ref, out_ref, a_ref, acc_ref):
    x = x_ref[...]
    y = _ffn_chunks(x.astype(BF16), None, (p2_ref, p1_ref), wup_ref, wdn_ref, cw_ref, cb_ref, acc_ref, a_ref, None)
    out_ref[...] = _layer_norm(DEEPNORM_ALPHA * x + y, g_ref[...], b_ref[...])


def _ffn_seq(x, wup, wdn, cw, cb, g, b, *, nb, tm):
    m = x.shape[0]
    nt = m // nb // tm
    hb = tm // V7X_SUBLANES
    row = pl.BlockSpec((tm, D_MODEL), lambda s, i: (s * nt + i, 0))
    halo = pl.BlockSpec((V7X_SUBLANES, D_MODEL), lambda s, i: (jnp.maximum((s * nt + i) * hb - 1, 0), 0))
    vec_d = pl.BlockSpec((1, D_MODEL), lambda s, i: (0, 0))
    return pl.pallas_call(
        _ffn_seq_kernel,
        grid=(nb, nt),
        in_specs=[row, halo, _resident(wup.shape), _resident(wdn.shape), _resident(cw.shape),
                  _resident((1, FFN_DIM)), vec_d, vec_d],
        out_specs=[row, pl.BlockSpec((1, V7X_SUBLANES, FFN_DIM), lambda s, i: (s, 0, 0))],
        out_shape=[jax.ShapeDtypeStruct((m, D_MODEL), F32),
                   jax.ShapeDtypeStruct((nb, V7X_SUBLANES, FFN_DIM), F32)],
        scratch_shapes=[pltpu.VMEM((tm, FFN_DIM), BF16)],
        compiler_params=_params("arbitrary", "arbitrary"),
        name="conv_ffn_seq",
    )(x, x, wup, wdn, cw, cb.reshape(1, -1), g.reshape(1, -1), b.reshape(1, -1))


def _ffn_state(x, p2, p1, wup, wdn, cw, cb, g, b):
    m = x.shape[0]
    full = lambda shape: pl.BlockSpec(shape, lambda i: (0,) * len(shape))
    return pl.pallas_call(
        _ffn_state_kernel,
        grid=(1,),
        in_specs=[full((m, D_MODEL)), full((m, FFN_DIM)), full((m, FFN_DIM)), _resident(wup.shape),
                  _resident(wdn.shape), _resident(cw.shape), _resident((1, FFN_DIM)),
                  full((1, D_MODEL)), full((1, D_MODEL))],
        out_specs=[full((m, D_MODEL)), full((m, FFN_DIM))],
        out_shape=[jax.ShapeDtypeStruct((m, D_MODEL), F32), jax.ShapeDtypeStruct((m, FFN_DIM), F32)],
        scratch_shapes=[pltpu.VMEM((m, FFN_DIM), BF16)],
        compiler_params=_params("arbitrary"),
        name="conv_ffn_state",
    )(x, p2, p1, wup, wdn, cw, cb.reshape(1, -1), g.reshape(1, -1), b.reshape(1, -1))


def _softmax_step(s, v, m_ref, l_ref, acc_ref):
    nl = s.shape[1] // V7X_LANES
    tiles = [s[:, j * V7X_LANES:(j + 1) * V7X_LANES] for j in range(nl)]
    m_prev = m_ref[...]
    m_new = jnp.maximum(m_prev, jnp.max(functools.reduce(jnp.maximum, tiles), axis=1, keepdims=True))
    alpha = jnp.exp(m_prev - m_new)
    ps = [jnp.exp(t - m_new) for t in tiles]
    l_ref[...] = alpha * l_ref[...] + functools.reduce(jnp.add, ps)
    p = jnp.concatenate(ps, axis=1).astype(BF16)
    acc_ref[...] = alpha * acc_ref[...] + jnp.dot(p, v, preferred_element_type=F32)
    m_ref[...] = m_new


def _diff_lambda(lam_ref, lam_init):
    lp = lam_ref[...]
    s1 = jnp.sum(lp[0:1, :] * lp[1:2, :], axis=1, keepdims=True)
    s2 = jnp.sum(lp[2:3, :] * lp[3:4, :], axis=1, keepdims=True)
    return jnp.exp(s1) - jnp.exp(s2) + lam_init


def _diff_finish(o, subln, lam_init):
    ms = jnp.mean(o * o, axis=-1, keepdims=True)
    return o * lax.rsqrt(ms + NORM_EPS) * subln * (1.0 - lam_init)


def _causal_keep(qi, koff, tq, tk, reps):
    qpos = qi * tq + lax.broadcasted_iota(jnp.int32, (tq, tk), 0)
    kpos = koff + lax.broadcasted_iota(jnp.int32, (tq, tk), 1)
    keep = jnp.where(kpos <= qpos, 1.0, 0.0)
    return jnp.concatenate([keep] * reps, axis=0) > 0.5


def _flash_sweep(qi, tq, tk, step):
    n_full = (qi * tq) // tk
    n_all = ((qi + 1) * tq + tk - 1) // tk
    lax.fori_loop(0, n_full // 2, lambda j, c: (step(j * (2 * tk), 2 * tk, False), c)[1], 0)

    @pl.when(n_full % 2 == 1)
    def _():
        step((n_full - 1) * tk, tk, False)

    lax.fori_loop(n_full, n_all, lambda j, c: (step(j * tk, tk, True), c)[1], 0)


def _diff_attn_kernel(lam_ref, subln_ref, q_ref, k_ref, v_ref, o_ref, m_s, l_s, a_s, *, tq, tk, lam_init):
    qi = pl.program_id(2)
    q = q_ref[0]
    lane = lax.broadcasted_iota(jnp.int32, q.shape, 1)
    zero = jnp.zeros_like(q)
    q12 = jnp.concatenate([jnp.where(lane < HEAD_DIM, q, zero), jnp.where(lane >= HEAD_DIM, q, zero)], axis=0)
    m_s[...] = jnp.full(m_s.shape, NEG_BIG, F32)
    l_s[...] = jnp.zeros(l_s.shape, F32)
    a_s[...] = jnp.zeros(a_s.shape, F32)
    nt_dims = (((1,), (1,)), ((), ()))

    def step(koff, width, masked):
        off = pl.multiple_of(koff, tk)
        s = lax.dot_general(q12, k_ref[0, pl.ds(off, width), :], nt_dims, preferred_element_type=F32)
        if masked:
            s = jnp.where(_causal_keep(qi, koff, tq, width, 2), s, NEG_BIG)
        _softmax_step(s, v_ref[0, pl.ds(off, width), :], m_s, l_s, a_s)

    _flash_sweep(qi, tq, tk, step)

    lam = _diff_lambda(lam_ref, lam_init)
    on = a_s[...] / jnp.sum(l_s[...], axis=1, keepdims=True)
    o = on[0:tq, :] - lam * on[tq:2 * tq, :]
    o_ref[0] = _diff_finish(o, subln_ref[...], lam_init).astype(BF16)


def _diff_attn(q, k, v, lam_p, subln, lam_init, *, tq, tk):
    nb, t, _ = q.shape
    qspec = pl.BlockSpec((1, tq, V7X_LANES), lambda b, h, i: (b, i, h))
    kvspec = pl.BlockSpec((1, t, V7X_LANES), lambda b, h, i: (b, 0, h))
    small = lambda shape: pl.BlockSpec(shape, lambda b, h, i: (0,) * len(shape))
    stat = pltpu.VMEM((2 * tq, V7X_LANES), F32)
    return pl.pallas_call(
        functools.partial(_diff_attn_kernel, tq=tq, tk=tk, lam_init=lam_init),
        grid=(nb, DIFF_HEADS, t // tq),
        in_specs=[small((4, HEAD_DIM)), small((1, 2 * HEAD_DIM)), qspec, kvspec, kvspec],
        out_specs=qspec,
        out_shape=jax.ShapeDtypeStruct((nb, t, D_MODEL), BF16),
        scratch_shapes=[stat, stat, stat],
        compiler_params=_params("arbitrary", "arbitrary", "arbitrary"),
        name="diff_attn",
    )(lam_p, subln.reshape(1, -1), q, k, v)


def _beats(row, g, tie_wins):
    return jnp.where(tie_wins, jnp.where(row >= g, 1.0, 0.0), jnp.where(row > g, 1.0, 0.0))


def _moba_select_bias(g_t, own, n_blocks):
    blk = lax.broadcasted_iota(jnp.int32, g_t.shape, 0)
    cnt = jnp.zeros(g_t.shape, F32)
    for n in range(n_blocks):
        cnt = cnt + _beats(g_t[n:n + 1, :], g_t, blk > n) * jnp.where(n < own, 1.0, 0.0)
    visible = jnp.where(cnt < MOBA_TOPK, 1.0, 0.0) * jnp.where(blk < own, 1.0, 0.0) + jnp.where(blk == own, 1.0, 0.0)
    return jnp.where(visible > 0.5, 0.0, NEG_BIG)


def _moba_attn_kernel(q_ref, k_ref, v_ref, mean_ref, o_ref, m_s, l_s, a_s, *, tq, tk, n_blocks):
    qi = pl.program_id(2)
    q = q_ref[0]
    lane = lax.broadcasted_iota(jnp.int32, q.shape, 1)
    zero = jnp.zeros_like(q)
    nt_dims = (((1,), (1,)), ((), ()))
    means = mean_ref[0].astype(BF16)
    own = (qi * tq + lax.broadcasted_iota(jnp.int32, (1, tq), 1)) // MOBA_BLOCK
    eye = (lax.broadcasted_iota(jnp.int32, (tq, tq), 0) == lax.broadcasted_iota(jnp.int32, (tq, tq), 1))
    eye = jnp.where(eye, 1.0, 0.0).astype(BF16)
    pad = jnp.zeros((V7X_LANES - n_blocks, tq), BF16)
    qs, sels = [], []
    for qq in (jnp.where(lane < HEAD_DIM, q, zero), jnp.where(lane >= HEAD_DIM, q, zero)):
        g_t = lax.dot_general(means, qq, nt_dims, preferred_element_type=F32)
        bias_t = jnp.concatenate([_moba_select_bias(g_t, own, n_blocks).astype(BF16), pad], axis=0)
        sels.append(lax.dot_general(eye, bias_t, nt_dims, preferred_element_type=F32).astype(BF16))
        qs.append(qq)
    q_aug = jnp.concatenate([jnp.concatenate(qs, axis=0), jnp.concatenate(sels, axis=0)], axis=1)
    m_s[...] = jnp.full(m_s.shape, NEG_BIG, F32)
    l_s[...] = jnp.zeros(l_s.shape, F32)
    a_s[...] = jnp.zeros(a_s.shape, F32)

    def step(koff, width, masked):
        off = pl.multiple_of(koff, tk)
        rel = (lax.broadcasted_iota(jnp.int32, (width, V7X_LANES), 1)
               - lax.broadcasted_iota(jnp.int32, (width, V7X_LANES), 0) // MOBA_BLOCK)
        onehot = jnp.where(rel == koff // MOBA_BLOCK, 1.0, 0.0).astype(BF16)
        k_aug = jnp.concatenate([k_ref[0, pl.ds(off, width), :], onehot], axis=1)
        s = lax.dot_general(q_aug, k_aug, nt_dims, preferred_element_type=F32)
        if masked:
            s = jnp.where(_causal_keep(qi, koff, tq, width, 2), s, NEG_BIG)
        _softmax_step(s, v_ref[0, pl.ds(off, width), :], m_s, l_s, a_s)

    _flash_sweep(qi, tq, tk, step)
    on = a_s[...] / jnp.sum(l_s[...], axis=1, keepdims=True)
    o_ref[0] = jnp.where(lane < HEAD_DIM, on[0:tq, :], on[tq:2 * tq, :]).astype(BF16)


def _moba_attn(q, k, v, kmeans, *, tq, tk):
    nb, t, _ = q.shape
    n_blocks = t // MOBA_BLOCK
    qspec = pl.BlockSpec((1, tq, V7X_LANES), lambda b, h, i: (b, i, h))
    kvspec = pl.BlockSpec((1, t, V7X_LANES), lambda b, h, i: (b, 0, h))
    mspec = pl.BlockSpec((1, n_blocks, V7X_LANES), lambda b, h, i: (b, 0, h))
    stat = pltpu.VMEM((2 * tq, V7X_LANES), F32)
    return pl.pallas_call(
        functools.partial(_moba_attn_kernel, tq=tq, tk=tk, n_blocks=n_blocks),
        grid=(nb, MOBA_HEADS // 2, t // tq),
        in_specs=[qspec, kvspec, kvspec, mspec],
        out_specs=qspec,
        out_shape=jax.ShapeDtypeStruct((nb, t, D_MODEL), BF16),
        scratch_shapes=[stat, stat, stat],
        compiler_params=_params("arbitrary", "arbitrary", "arbitrary"),
        name="moba_attn",
    )(q, k, v, kmeans)


def _softplus(z):
    return jnp.maximum(z, 0.0) + jnp.log1p(jnp.exp(-jnp.abs(z)))


def _rg_gates(uc, wa_ref, ba_ref, wi_ref, bi_ref, lam_ref, a_ref, b_ref):
    sp = _softplus(-lam_ref[...])
    for n in range(RG_BLOCKS):
        sl = slice(n * RG_BW, (n + 1) * RG_BW)
        ub = uc[:, sl]
        ubb = ub.astype(BF16)
        r = jax.nn.sigmoid(jnp.dot(ubb, wa_ref[n], preferred_element_type=F32) + ba_ref[:, sl])
        ig = jax.nn.sigmoid(jnp.dot(ubb, wi_ref[n], preferred_element_type=F32) + bi_ref[:, sl])
        log_a = -RG_C * r * sp[:, sl]
        a = jnp.exp(log_a)
        a_ref[:, sl] = a
        b_ref[:, sl] = jnp.sqrt(1.0 - jnp.exp(2.0 * log_a)) * (ig * ub)


def _rg_seq_kernel(x_ref, xh_ref, win_ref, cw_ref, cb_ref, wa_ref, ba_ref, wi_ref, bi_ref, lam_ref,
                   y_ref, hl_ref, tail_ref, gate_s, a_s, b_s, h_s, carry_s):
    i = pl.program_id(1)
    tm = x_ref.shape[0]
    w = RG_WIDTH
    xb = x_ref[...].astype(BF16)
    xh = jnp.where(i == 0, 0.0, xh_ref[...]).astype(BF16)
    gate_s[...] = jnp.dot(xb, win_ref[:, 0:w], preferred_element_type=F32)
    u = jnp.dot(xb, win_ref[:, w:2 * w], preferred_element_type=F32)
    uh = jnp.dot(xh, win_ref[:, w:2 * w], preferred_element_type=F32)
    cw = cw_ref[...]
    uc = cb_ref[...] + u * cw[3:4, :]
    for s in range(1, RG_CONV):
        uc = uc + _shift_rows(u, uh, s) * cw[3 - s:4 - s, :]
    _rg_gates(uc, wa_ref, ba_ref, wi_ref, bi_ref, lam_ref, a_s, b_s)

    @pl.when(i == 0)
    def _():
        carry_s[...] = jnp.zeros_like(carry_s)

    row = lax.broadcasted_iota(jnp.int32, (V7X_SUBLANES, w), 0)

    def group(gi, c):
        off = pl.multiple_of(gi * V7X_SUBLANES, V7X_SUBLANES)
        a8 = a_s[pl.ds(off, V7X_SUBLANES), :]
        b8 = b_s[pl.ds(off, V7X_SUBLANES), :]
        for s in (1, 2, 4):
            ok = row >= s
            b8 = jnp.where(ok, a8 * pltpu.roll(b8, s, axis=0) + b8, b8)
            a8 = jnp.where(ok, a8 * pltpu.roll(a8, s, axis=0), a8)
        h8 = a8 * carry_s[V7X_SUBLANES - 1:V7X_SUBLANES, :] + b8
        h_s[pl.ds(off, V7X_SUBLANES), :] = h8
        carry_s[...] = h8
        return c

    lax.fori_loop(0, tm // V7X_SUBLANES, group, 0)
    y_ref[...] = (jax.nn.gelu(gate_s[...], approximate=True) * h_s[...]).astype(BF16)

    @pl.when(i == pl.num_programs(1) - 1)
    def _():
        hl_ref[0] = carry_s[...]
        tail_ref[0] = u[tm - V7X_SUBLANES:tm, :]


def _rg_seq(x, win, cw, cb, wa, ba, wi, bi, lam, *, nb, tm):
    m = x.shape[0]
    nt = m // nb // tm
    hb = tm // V7X_SUBLANES
    w = RG_WIDTH
    row = pl.BlockSpec((tm, D_MODEL), lambda s, i: (s * nt + i, 0))
    halo = pl.BlockSpec((V7X_SUBLANES, D_MODEL), lambda s, i: (jnp.maximum((s * nt + i) * hb - 1, 0), 0))
    vec = _resident((1, w))
    per_seq = pl.BlockSpec((1, V7X_SUBLANES, w), lambda s, i: (s, 0, 0))
    big = pltpu.VMEM((tm, w), F32)
    return pl.pallas_call(
        _rg_seq_kernel,
        grid=(nb, nt),
        in_specs=[row, halo, _resident(win.shape), _resident(cw.shape), vec, _resident(wa.shape), vec,
                  _resident(wi.shape), vec, vec],
        out_specs=[pl.BlockSpec((tm, w), lambda s, i: (s * nt + i, 0)), per_seq, per_seq],
        out_shape=[jax.ShapeDtypeStruct((m, w), BF16), jax.ShapeDtypeStruct((nb, V7X_SUBLANES, w), F32),
                   jax.ShapeDtypeStruct((nb, V7X_SUBLANES, w), F32)],
        scratch_shapes=[big, big, big, big, pltpu.VMEM((V7X_SUBLANES, w), F32)],
        compiler_params=_params("arbitrary", "arbitrary"),
        name="rglru_seq",
    )(x, x, win, cw, cb.reshape(1, -1), wa, ba.reshape(1, -1), wi, bi.reshape(1, -1), lam.reshape(1, -1))


def _rg_state_kernel(x_ref, h0_ref, c0_ref, c1_ref, c2_ref, win_ref, cw_ref, cb_ref, wa_ref, ba_ref, wi_ref, bi_ref,
                     lam_ref, y_ref, h_ref, u_ref, a_s, b_s):
    w = RG_WIDTH
    xb = x_ref[...].astype(BF16)
    gate = jnp.dot(xb, win_ref[:, 0:w], preferred_element_type=F32)
    u = jnp.dot(xb, win_ref[:, w:2 * w], preferred_element_type=F32)
    cw = cw_ref[...]
    uc = cb_ref[...] + c0_ref[...] * cw[0:1, :] + c1_ref[...] * cw[1:2, :] + c2_ref[...] * cw[2:3, :] + u * cw[3:4, :]
    _rg_gates(uc, wa_ref, ba_ref, wi_ref, bi_ref, lam_ref, a_s, b_s)
    h = a_s[...] * h0_ref[...] + b_s[...]
    h_ref[...] = h
    u_ref[...] = u
    y_ref[...] = (jax.nn.gelu(gate, approximate=True) * h).astype(BF16)


def _rg_state(x, h0, c0, c1, c2, win, cw, cb, wa, ba, wi, bi, lam):
    m = x.shape[0]
    w = RG_WIDTH
    full = lambda shape: pl.BlockSpec(shape, lambda i: (0,) * len(shape))
    st = full((m, w))
    vec = full((1, w))
    return pl.pallas_call(
        _rg_state_kernel,
        grid=(1,),
        in_specs=[full((m, D_MODEL)), st, st, st, st, _resident(win.shape), full(cw.shape), vec,
                  _resident(wa.shape), vec, _resident(wi.shape), vec, vec],
        out_specs=[st, st, st],
        out_shape=[jax.ShapeDtypeStruct((m, w), BF16), jax.ShapeDtypeStruct((m, w), F32),
                   jax.ShapeDtypeStruct((m, w), F32)],
        scratch_shapes=[pltpu.VMEM((m, w), F32), pltpu.VMEM((m, w), F32)],
        compiler_params=_params("arbitrary"),
        name="rglru_state",
    )(x, h0, c0, c1, c2, win, cw, cb.reshape(1, -1), wa, ba.reshape(1, -1), wi, bi.reshape(1, -1), lam.reshape(1, -1))


def _page_scores(k_refs, q_b):
    return jnp.concatenate([jnp.sum(kr[0] * q_b, axis=1) for kr in k_refs], axis=1)


def _diff_decode_kernel(pt_ref, lam_ref, subln_ref, qb_ref, q_ref, kn_ref, vn_ref, *rest, n_pg, lam_init):
    del pt_ref
    k_refs, v_refs = rest[:n_pg], rest[n_pg:2 * n_pg]
    o_ref, m_s, l_s, a_s = rest[2 * n_pg:]
    step = pl.program_id(1)
    h = DIFF_HEADS

    @pl.when(step == 0)
    def _():
        s_new = jnp.sum(kn_ref[0] * q_ref[0], axis=-1, keepdims=True)
        m_s[...] = jnp.broadcast_to(s_new, m_s.shape)
        l_s[...] = jnp.full(l_s.shape, 1.0 / V7X_LANES, F32)
        a_s[...] = vn_ref[0]

    s = _page_scores(k_refs, qb_ref[0])
    m_prev = m_s[...]
    m_new = jnp.maximum(m_prev, jnp.max(s, axis=1, keepdims=True))
    alpha = jnp.exp(m_prev - m_new)
    p = jnp.exp(s - jnp.concatenate([m_new] * n_pg, axis=1))
    l_s[...] = alpha * l_s[...] + functools.reduce(
        jnp.add, [p[:, g * V7X_LANES:(g + 1) * V7X_LANES] for g in range(n_pg)])
    pb = p.astype(BF16)
    wide = PAGE_SIZE * h
    col = lax.broadcasted_iota(jnp.int32, (PAGE_SIZE, wide), 1)
    spread = jnp.where(col // h == lax.broadcasted_iota(jnp.int32, (PAGE_SIZE, wide), 0), 1.0, 0.0).astype(BF16)
    mine = (lax.broadcasted_iota(jnp.int32, (2 * h, wide), 1) % h) == (lax.broadcasted_iota(jnp.int32, (2 * h, wide), 0) // 2)
    pv = jnp.zeros(a_s.shape, F32)
    for g, vr in enumerate(v_refs):
        p_wide = jnp.dot(pb[:, g * V7X_LANES:(g + 1) * V7X_LANES], spread, preferred_element_type=F32)
        p_wide = jnp.where(mine, p_wide, 0.0).astype(BF16)
        v2 = vr[0].reshape(wide, 2 * HEAD_DIM).astype(BF16)
        pv = pv + jnp.dot(p_wide, v2, preferred_element_type=F32)
    a_s[...] = alpha * a_s[...] + pv
    m_s[...] = m_new

    @pl.when(step == pl.num_programs(1) - 1)
    def _():
        lam = _diff_lambda(lam_ref, lam_init)
        a_s[...] = a_s[...] / jnp.sum(l_s[...], axis=1, keepdims=True)
        o = a_s[pl.ds(0, h, stride=2), :] - lam * a_s[pl.ds(1, h, stride=2), :]
        o_ref[0] = _diff_finish(o, subln_ref[...], lam_init).astype(BF16)


def _decode_specs(page_table, n_pg, kshape, vshape):
    per_seq = lambda shape: pl.BlockSpec((1,) + shape, lambda b, s, pt: (b,) + (0,) * len(shape))

    def page(shape, g):
        return pl.BlockSpec((1,) + shape, lambda b, s, pt: (pt[b, s * n_pg + g],) + (0,) * len(shape))

    return per_seq, [page(kshape, g) for g in range(n_pg)] + [page(vshape, g) for g in range(n_pg)]


def _diff_decode(q, k_new, v_new, cache_k, cache_v, page_table, lam_p, subln, lam_init):
    nb, n_pages = page_table.shape
    n_pg = math.gcd(DECODE_PAGES, n_pages)
    h, c = DIFF_HEADS, 2 * DIFF_HEADS
    k_t = jnp.transpose(cache_k, (0, 2, 3, 1))
    q_b = jnp.broadcast_to(q[..., None], (nb, c, HEAD_DIM, PAGE_SIZE))
    v_rows = jnp.repeat(v_new, 2, axis=1)
    per_seq, pages = _decode_specs(page_table, n_pg, (c, HEAD_DIM, PAGE_SIZE), (PAGE_SIZE, h, 2 * HEAD_DIM))
    small = lambda shape: pl.BlockSpec(shape, lambda b, s, pt: (0,) * len(shape))
    stat = pltpu.VMEM((c, V7X_LANES), F32)
    return pl.pallas_call(
        functools.partial(_diff_decode_kernel, n_pg=n_pg, lam_init=lam_init),
        grid_spec=pltpu.PrefetchScalarGridSpec(
            num_scalar_prefetch=1,
            grid=(nb, n_pages // n_pg),
            in_specs=[small((4, HEAD_DIM)), small((1, 2 * HEAD_DIM)), per_seq((c, HEAD_DIM, PAGE_SIZE)),
                      per_seq((c, HEAD_DIM)), per_seq((c, HEAD_DIM)), per_seq((c, 2 * HEAD_DIM))] + pages,
            out_specs=per_seq((h, 2 * HEAD_DIM)),
            scratch_shapes=[stat, stat, stat],
        ),
        out_shape=jax.ShapeDtypeStruct((nb, h, 2 * HEAD_DIM), BF16),
        compiler_params=_params("arbitrary", "arbitrary"),
        name="diff_decode",
    )(page_table, lam_p, subln.reshape(1, -1), q_b, q, k_new, v_rows, *([k_t] * n_pg), *([cache_v] * n_pg))


def _moba_decode_kernel(pt_ref, qb_ref, q_ref, kn_ref, vn_ref, *rest, n_pg, n_pages):
    del pt_ref
    k_refs, v_refs = rest[:n_pg], rest[n_pg:2 * n_pg]
    o_ref, s_s, m_s, l_s, r_s = rest[2 * n_pg:]
    step = pl.program_id(1)
    c = MOBA_HEADS
    nt_dims = (((1,), (1,)), ((), ()))
    s = _page_scores(k_refs, qb_ref[0])
    for g in range(n_pg):
        sg = s[:, g * V7X_LANES:(g + 1) * V7X_LANES]
        mg = jnp.broadcast_to(jnp.max(sg, axis=1, keepdims=True), sg.shape)
        pg = jnp.exp(sg - mg)
        page = step * n_pg + g
        s_s[page] = sg
        m_s[page] = mg
        l_s[page] = pg
        v2 = v_refs[g][0].reshape(c * HEAD_DIM, PAGE_SIZE).astype(BF16)
        r_s[page] = lax.dot_general(pg.astype(BF16), v2, nt_dims, preferred_element_type=F32)

    @pl.when(step == pl.num_programs(1) - 1)
    def _():
        pages_per_block = MOBA_BLOCK // PAGE_SIZE
        nblk = n_pages // pages_per_block
        wide = (n_pages, c, V7X_LANES)
        gsum = jnp.sum(s_s[...], axis=2, keepdims=True).reshape(nblk, pages_per_block, c, 1)
        gate = jnp.broadcast_to(jnp.sum(gsum, axis=1), (nblk, c, V7X_LANES))
        cnt = jnp.zeros(gate.shape, F32)
        blk = lax.broadcasted_iota(jnp.int32, gate.shape, 0)
        for n in range(nblk):
            cnt = cnt + _beats(gate[n:n + 1], gate, blk > n)
        selw = jnp.where(cnt < MOBA_TOPK, 1.0, 0.0)
        selw = jnp.broadcast_to(selw[:, None], (nblk, pages_per_block, c, V7X_LANES)).reshape(wide)
        sel = selw > 0.5
        s_new = jnp.broadcast_to(jnp.sum(kn_ref[0] * q_ref[0], axis=-1, keepdims=True), (c, V7X_LANES))
        mp = jnp.where(sel, m_s[...], NEG_BIG)
        m_all = jnp.maximum(jnp.max(mp, axis=0), s_new)
        w = jnp.where(sel, jnp.exp(mp - m_all[None]), 0.0)
        w_new = jnp.exp(s_new - m_all)
        lsum = jnp.sum(l_s[...], axis=2, keepdims=True)
        den = jnp.sum(w * lsum, axis=0) + w_new
        reps = c * HEAD_DIM // V7X_LANES
        num = jnp.sum(jnp.concatenate([w] * reps, axis=2) * r_s[...], axis=0)
        num = num + jnp.concatenate([w_new] * reps, axis=1) * vn_ref[0]
        row = lax.broadcasted_iota(jnp.int32, (c, HEAD_DIM), 0)
        out = jnp.zeros((c, HEAD_DIM), F32)
        for cc in range(c):
            out = jnp.where(row == cc, num[:, cc * HEAD_DIM:(cc + 1) * HEAD_DIM], out)
        o_ref[0] = (out / den[:, 0:HEAD_DIM]).astype(BF16)


def _moba_decode(q, k_new, v_new, cache_k, cache_v, page_table):
    nb, n_pages = page_table.shape
    n_pg = math.gcd(DECODE_PAGES, n_pages)
    c = MOBA_HEADS
    k_t = jnp.transpose(cache_k, (0, 2, 3, 1))
    v_t = jnp.transpose(cache_v, (0, 2, 3, 1))
    q_b = jnp.broadcast_to(q[..., None], (nb, c, HEAD_DIM, PAGE_SIZE))
    v_wide = jnp.tile(v_new, (1, 1, c))
    pshape = (c, HEAD_DIM, PAGE_SIZE)
    per_seq, pages = _decode_specs(page_table, n_pg, pshape, pshape)
    part = pltpu.VMEM((n_pages, c, V7X_LANES), F32)
    return pl.pallas_call(
        functools.partial(_moba_decode_kernel, n_pg=n_pg, n_pages=n_pages),
        grid_spec=pltpu.PrefetchScalarGridSpec(
            num_scalar_prefetch=1,
            grid=(nb, n_pages // n_pg),
            in_specs=[per_seq(pshape), per_seq((c, HEAD_DIM)), per_seq((c, HEAD_DIM)),
                      per_seq((c, c * HEAD_DIM))] + pages,
            out_specs=per_seq((c, HEAD_DIM)),
            scratch_shapes=[part, part, part, pltpu.VMEM((n_pages, c, c * HEAD_DIM), F32)],
        ),
        out_shape=jax.ShapeDtypeStruct((nb, c, HEAD_DIM), BF16),
        compiler_params=_params("arbitrary", "arbitrary"),
        name="moba_decode",
    )(page_table, q_b, q, k_new, v_wide, *([k_t] * n_pg), *([v_t] * n_pg))


def kernel(x_prompt, x_sample, cache_k_l0, cache_v_l0, cache_k_l1, cache_v_l1, cache_k_l3, cache_v_l3, state_rg_h_l2, state_rg_conv_l2, state_ffn_conv, page_table, l0_w_qkv, l0_lambda, l0_subln, l0_w_o, l1_w_qkv, l1_w_o, l2_w_in, l2_conv_w, l2_conv_b, l2_w_a, l2_b_a, l2_w_i, l2_b_i, l2_lambda, l2_w_o, l3_w_qkv, l3_lambda, l3_subln, l3_w_o, ffn_w_up, ffn_conv_w, ffn_conv_b, ffn_w_down, ln_mix_g, ln_mix_b, ln_ffn_g, ln_ffn_b):
    nb, t, d = x_prompt.shape
    ns = x_sample.shape[0]
    past_len = page_table.shape[1] * PAGE_SIZE
    tm = min(ROW_TILE, t)
    tab_p = _rope_tables(jnp.arange(t, dtype=jnp.int32))
    tab_s = _rope_tables(jnp.full((ns,), past_len, dtype=jnp.int32))
    xp = x_prompt.reshape(nb * t, d)
    xs = x_sample.reshape(ns, d)
    outs_p, outs_s, ffn_p, ffn_s = {}, {}, [], []

    def attn_layer(i, w_qkv, w_o, cache_k, cache_v, lam_p=None, subln=None):
        nonlocal xp, xs
        wq = w_qkv.astype(BF16)
        wo = w_o.astype(BF16)
        diff = lam_p is not None
        res_p = _qkv_rope(xp, wq, tab_p, tm=tm, rows_per_seq=t, with_means=not diff, k_t=True, v_t=not diff)
        q_p, kb_p, vb_p, k_p, v_p = res_p[:5]
        q_s, _, _, k_s, v_s = _qkv_rope(xs, wq, tab_s, tm=ns, rows_per_seq=ns, with_means=False)
        shp = lambda a: a.reshape(nb, t, d)
        q_s3 = q_s.astype(F32).reshape(ns, 2 * DIFF_HEADS, HEAD_DIM)
        k_s3 = k_s.reshape(ns, 2 * DIFF_HEADS, HEAD_DIM)
        if diff:
            lam_init = 0.8 - 0.6 * math.exp(-0.3 * i)
            o_p = _diff_attn(shp(q_p), shp(kb_p), shp(vb_p), lam_p, subln, lam_init, tq=min(ATT_TQ, t), tk=min(ATT_TK, t))
            v_s3 = v_s.reshape(ns, DIFF_HEADS, 2 * HEAD_DIM)
            o_s = _diff_decode(q_s3, k_s3, v_s3, cache_k, cache_v, page_table, lam_p, subln, lam_init)
            kv_shape = ((2 * DIFF_HEADS, HEAD_DIM), (DIFF_HEADS, 2 * HEAD_DIM))
        else:
            kmeans = res_p[5].reshape(nb, t // MOBA_BLOCK, d)
            o_p = _moba_attn(shp(q_p), shp(kb_p), shp(vb_p), kmeans, tq=min(ATT_TQ, t), tk=min(ATT_TK, t))
            v_s3 = v_s.reshape(ns, MOBA_HEADS, HEAD_DIM)
            o_s = _moba_decode(q_s3, k_s3, v_s3, cache_k, cache_v, page_table)
            kv_shape = ((MOBA_HEADS, HEAD_DIM), (MOBA_HEADS, HEAD_DIM))
        back = lambda a: jnp.transpose(a, (0, 3, 1, 2))
        outs_p[i] = (back(k_p), v_p.reshape(nb, t, *kv_shape[1]) if diff else back(v_p))
        outs_s[i] = (k_s.reshape(ns, 1, *kv_shape[0]), v_s.reshape(ns, 1, *kv_shape[1]))
        xp = _proj_ln(o_p.reshape(nb * t, d), xp, wo, ln_mix_g[i], ln_mix_b[i], tm=tm)
        xs = _proj_ln(o_s.reshape(ns, d), xs, wo, ln_mix_g[i], ln_mix_b[i], tm=ns)

    def rg_layer(i):
        nonlocal xp, xs
        win = l2_w_in.astype(BF16)
        wa = l2_w_a.astype(BF16)
        wi = l2_w_i.astype(BF16)
        wo = l2_w_o.astype(BF16)
        y_p, hl_p, tail_p = _rg_seq(xp, win, l2_conv_w, l2_conv_b, wa, l2_b_a, wi, l2_b_i, l2_lambda, nb=nb, tm=tm)
        cs = state_rg_conv_l2
        y_s, h_s, u_s = _rg_state(xs, state_rg_h_l2, cs[:, 0], cs[:, 1], cs[:, 2], win, l2_conv_w, l2_conv_b,
                                  wa, l2_b_a, wi, l2_b_i, l2_lambda)
        outs_p[i] = (hl_p[:, V7X_SUBLANES - 1], tail_p[:, V7X_SUBLANES - (RG_CONV - 1):])
        outs_s[i] = (h_s, jnp.concatenate([cs[:, 1:], u_s[:, None]], axis=1))
        xp = _proj_ln(y_p, xp, wo, ln_mix_g[i], ln_mix_b[i], tm=tm)
        xs = _proj_ln(y_s, xs, wo, ln_mix_g[i], ln_mix_b[i], tm=ns)

    def ffn_layer(i):
        nonlocal xp, xs
        wup = ffn_w_up[i].astype(BF16)
        wdn = ffn_w_down[i].astype(BF16)
        xp, tail = _ffn_seq(xp, wup, wdn, ffn_conv_w[i], ffn_conv_b[i], ln_ffn_g[i], ln_ffn_b[i], nb=nb,
                            tm=min(FFN_ROW_TILE, t))
        st = state_ffn_conv[i]
        xs, a_s = _ffn_state(xs, st[:, 0], st[:, 1], wup, wdn, ffn_conv_w[i], ffn_conv_b[i], ln_ffn_g[i], ln_ffn_b[i])
        ffn_p.append(tail[:, V7X_SUBLANES - (FFN_CONV - 1):])
        ffn_s.append(jnp.concatenate([st[:, 1:], a_s[:, None]], axis=1))

    attn_layer(0, l0_w_qkv, l0_w_o, cache_k_l0, cache_v_l0, l0_lambda, l0_subln)
    ffn_layer(0)
    attn_layer(1, l1_w_qkv, l1_w_o, cache_k_l1, cache_v_l1)
    ffn_layer(1)
    rg_layer(2)
    ffn_layer(2)
    attn_layer(3, l3_w_qkv, l3_w_o, cache_k_l3, cache_v_l3, l3_lambda, l3_subln)
    ffn_layer(3)

    return (xp.reshape(nb, t, d), xs.reshape(ns, 1, d),
            outs_p[0][0], outs_p[0][1], outs_p[1][0], outs_p[1][1], outs_p[2][0], outs_p[2][1],
            outs_p[3][0], outs_p[3][1], jnp.stack(ffn_p),
            outs_s[0][0], outs_s[0][1], outs_s[1][0], outs_s[1][1], outs_s[2][0], outs_s[2][1],
            outs_s[3][0], outs_s[3][1], jnp.stack(ffn_s))
```

```python
import functools
import math

import jax
import jax.numpy as jnp
from jax import lax
from jax.experimental import pallas as pl
from jax.experimental.pallas import tpu as pltpu

F32 = jnp.float32
BF16 = jnp.bfloat16

D_MODEL = 1024
DEPTH = 4
PAGE_SIZE = 128
HEAD_DIM = 64
DIFF_HEADS = D_MODEL // (2 * HEAD_DIM)
MOBA_HEADS = D_MODEL // HEAD_DIM
MOBA_BLOCK = 256
MOBA_TOPK = 3
RG_WIDTH = 3 * D_MODEL // 2
RG_BLOCKS = 12
RG_BW = RG_WIDTH // RG_BLOCKS
RG_CONV = 4
RG_C = 8.0
FFN_DIM = 2816
FFN_CONV = 3
ROPE_THETA = 10000.0
NORM_EPS = 1e-5
DEEPNORM_ALPHA = (2.0 * DEPTH) ** 0.25

V7X_LANES = 128
V7X_SUBLANES = 8
V7X_VMEM_BYTES = 64 * 1024 * 1024
VMEM_LIMIT = V7X_VMEM_BYTES * 7 // 8

NEG_BIG = -1e30

ROW_TILE = 512
FFN_ROW_TILE = 1024
FFN_CHUNK = 512
ATT_TQ = 512
DECODE_PAGES = 8


def _params(*sem):
    return pltpu.CompilerParams(dimension_semantics=sem, vmem_limit_bytes=VMEM_LIMIT)


def _resident(shape):
    nd = len(shape)
    return pl.BlockSpec(shape, lambda *_: (0,) * nd, pipeline_mode=pl.Buffered(1))


def _layer_norm(z, g, b):
    mu = jnp.mean(z, axis=-1, keepdims=True)
    zc = z - mu
    var = jnp.mean(zc * zc, axis=-1, keepdims=True)
    return zc * lax.rsqrt(var + NORM_EPS) * g + b


def _shift_rows(a, halo, s):
    row = lax.broadcasted_iota(jnp.int32, a.shape, 0)
    out = pltpu.roll(a, s, axis=0)
    hn = halo.shape[0]
    for r in range(s):
        out = jnp.where(row == r, halo[hn - s + r:hn - s + r + 1, :], out)
    return out


def _rope_cols(t, cos, sa, sb):
    return t * cos + pltpu.roll(t, HEAD_DIM // 2, axis=1) * sa + pltpu.roll(t, V7X_LANES - HEAD_DIM // 2, axis=1) * sb


def _store_heads(out_ref, cols, x, transposed):
    if transposed:
        pair = V7X_LANES // HEAD_DIM
        first = cols.start // HEAD_DIM
        out_ref[0, first:first + pair, :, :] = x.T.reshape(pair, HEAD_DIM, x.shape[0])
    else:
        out_ref[:, cols] = x


def _qkv_rope_kernel(x_ref, w_ref, cos_ref, sa_ref, sb_ref, q_ref, kb_ref, vb_ref, k_ref, v_ref, *rest,
                     q_scale, n_mean_blocks, k_t, v_t):
    xb = x_ref[...].astype(BF16)
    cos, sa, sb = cos_ref[...], sa_ref[...], sb_ref[...]
    d = D_MODEL
    q = jnp.dot(xb, w_ref[:, 0:d], preferred_element_type=F32)
    k = jnp.dot(xb, w_ref[:, d:2 * d], preferred_element_type=F32)
    v = jnp.dot(xb, w_ref[:, 2 * d:3 * d], preferred_element_type=F32)
    vb_ref[...] = v.astype(BF16)
    sums = [[] for _ in range(n_mean_blocks)]
    for c in range(d // V7X_LANES):
        sl = slice(c * V7X_LANES, (c + 1) * V7X_LANES)
        qc = _rope_cols(q[:, sl], cos, sa, sb)
        kc = _rope_cols(k[:, sl], cos, sa, sb)
        q_ref[:, sl] = (qc * q_scale).astype(BF16)
        kb_ref[:, sl] = kc.astype(BF16)
        _store_heads(k_ref, sl, kc, k_t)
        _store_heads(v_ref, sl, v[:, sl], v_t)
        for r in range(n_mean_blocks):
            sums[r].append(jnp.mean(kc[r * MOBA_BLOCK:(r + 1) * MOBA_BLOCK, :], axis=0, keepdims=True))
    for r in range(n_mean_blocks):
        rest[0][r] = jnp.concatenate(sums[r], axis=1)


def _rope_tables(pos):
    half = HEAD_DIM // 2
    inv_freq = ROPE_THETA ** (-jnp.arange(half, dtype=F32) / half)
    ang = pos.astype(F32)[:, None] * inv_freq[None, :]
    cos = jnp.cos(ang)
    sin = jnp.sin(ang)
    zero = jnp.zeros_like(sin)
    cos128 = jnp.tile(cos, (1, 4))
    sa128 = jnp.tile(jnp.concatenate([zero, sin], axis=1), (1, 2))
    sb128 = jnp.tile(jnp.concatenate([-sin, zero], axis=1), (1, 2))
    return cos128, sa128, sb128


def _qkv_rope(x, w, tables, *, tm, rows_per_seq, with_means, k_t=False, v_t=False):
    m = x.shape[0]
    nt = rows_per_seq // tm
    n_mean = tm // MOBA_BLOCK if with_means else 0
    row = pl.BlockSpec((tm, D_MODEL), lambda i: (i, 0))
    tab = pl.BlockSpec((tm, V7X_LANES), lambda i: (i % nt, 0))
    heads = D_MODEL // HEAD_DIM
    row_f32 = jax.ShapeDtypeStruct((m, D_MODEL), F32)
    t_f32 = jax.ShapeDtypeStruct((m // rows_per_seq, heads, HEAD_DIM, rows_per_seq), F32)
    t_spec = pl.BlockSpec((1, heads, HEAD_DIM, tm), lambda i: (i // nt, 0, 0, i % nt))
    bf = jax.ShapeDtypeStruct((m, D_MODEL), BF16)
    out_shape = [bf, bf, bf, t_f32 if k_t else row_f32, t_f32 if v_t else row_f32]
    out_specs = [row, row, row, t_spec if k_t else row, t_spec if v_t else row]
    if with_means:
        out_shape.append(jax.ShapeDtypeStruct((m // MOBA_BLOCK, 1, D_MODEL), F32))
        out_specs.append(pl.BlockSpec((n_mean, 1, D_MODEL), lambda i: (i, 0, 0)))
    return pl.pallas_call(
        functools.partial(_qkv_rope_kernel, q_scale=HEAD_DIM ** -0.5, n_mean_blocks=n_mean, k_t=k_t, v_t=v_t),
        grid=(m // tm,),
        in_specs=[row, _resident(w.shape), tab, tab, tab],
        out_specs=out_specs,
        out_shape=out_shape,
        compiler_params=_params("arbitrary"),
        name="qkv_rope",
    )(x, w, *tables)


def _proj_ln_kernel(o_ref, x_ref, w_ref, g_ref, b_ref, out_ref):
    y = jnp.dot(o_ref[...], w_ref[...], preferred_element_type=F32)
    out_ref[...] = _layer_norm(DEEPNORM_ALPHA * x_ref[...] + y, g_ref[...], b_ref[...])


def _proj_ln(o, x, w, g, b, *, tm):
    m, kd = o.shape
    row_o = pl.BlockSpec((tm, kd), lambda i: (i, 0))
    row_x = pl.BlockSpec((tm, D_MODEL), lambda i: (i, 0))
    vec = pl.BlockSpec((1, D_MODEL), lambda i: (0, 0))
    return pl.pallas_call(
        _proj_ln_kernel,
        grid=(m // tm,),
        in_specs=[row_o, row_x, _resident(w.shape), vec, vec],
        out_specs=row_x,
        out_shape=jax.ShapeDtypeStruct((m, D_MODEL), F32),
        compiler_params=_params("arbitrary"),
        name="proj_ln",
    )(o, x, w, g.reshape(1, -1), b.reshape(1, -1))


def _ffn_chunks(xb, xh, p_refs, wup_ref, wdn_ref, cw_ref, cb_ref, acc_ref, a_out_ref, write_tail):
    tm = xb.shape[0]

    def chunk(off, width):
        wa = wup_ref[:, off:off + width]
        wg = wup_ref[:, FFN_DIM + off:FFN_DIM + off + width]
        a = jnp.dot(xb, wa, preferred_element_type=F32)
        g = jnp.dot(xb, wg, preferred_element_type=F32)
        if xh is not None:
            ah = jnp.dot(xh, wa, preferred_element_type=F32)
            p1 = _shift_rows(a, ah, 1)
            p2 = _shift_rows(a, ah, 2)
        else:
            p2 = p_refs[0][:, off:off + width]
            p1 = p_refs[1][:, off:off + width]
        cw = cw_ref[:, off:off + width]
        cb = cb_ref[:, off:off + width]
        ac = cb + p2 * cw[0:1, :] + p1 * cw[1:2, :] + a * cw[2:3, :]
        acc_ref[:, off:off + width] = (jax.nn.gelu(ac, approximate=True) * g).astype(BF16)
        if write_tail is None:
            a_out_ref[:, off:off + width] = a
        else:
            @pl.when(write_tail)
            def _():
                a_out_ref[0, :, off:off + width] = a[tm - V7X_SUBLANES:tm, :]

    for off in range(0, FFN_DIM, FFN_CHUNK):
        chunk(off, min(FFN_CHUNK, FFN_DIM - off))
    return jnp.dot(acc_ref[...], wdn_ref[...], preferred_element_type=F32)


def _ffn_seq_kernel(x_ref, xh_ref, wup_ref, wdn_ref, cw_ref, cb_ref, g_ref, b_ref, out_ref, tail_ref, acc_ref):
    i = pl.program_id(1)
    x = x_ref[...]
    xb = x.astype(BF16)
    xh = jnp.where(i == 0, 0.0, xh_ref[...]).astype(BF16)
    y = _ffn_chunks(xb, xh, None, wup_ref, wdn_ref, cw_ref, cb_ref, acc_ref, tail_ref, i == pl.num_programs(1) - 1)
    out_ref[...] = _layer_norm(DEEPNORM_ALPHA * x + y, g_ref[...], b_ref[...])


def _ffn_state_kernel(x_ref, p2_ref, p1_ref, wup_ref, wdn_ref, cw_ref, cb_ref, g_ref, b_ref, out_ref, a_ref, acc_ref):
    x = x_ref[...]
    y = _ffn_chunks(x.astype(BF16), None, (p2_ref, p1_ref), wup_ref, wdn_ref, cw_ref, cb_ref, acc_ref, a_ref, None)
    out_ref[...] = _layer_norm(DEEPNORM_ALPHA * x + y, g_ref[...], b_ref[...])


def _ffn_seq(x, wup, wdn, cw, cb, g, b, *, nb, tm):
    m = x.shape[0]
    nt = m // nb // tm
    hb = tm // V7X_SUBLANES
    row = pl.BlockSpec((tm, D_MODEL), lambda s, i: (s * nt + i, 0))
    halo = pl.BlockSpec((V7X_SUBLANES, D_MODEL), lambda s, i: (jnp.maximum((s * nt + i) * hb - 1, 0), 0))
    vec_d = pl.BlockSpec((1, D_MODEL), lambda s, i: (0, 0))
    return pl.pallas_call(
        _ffn_seq_kernel,
        grid=(nb, nt),
        in_specs=[row, halo, _resident(wup.shape), _resident(wdn.shape), _resident(cw.shape),
                  _resident((1, FFN_DIM)), vec_d, vec_d],
        out_specs=[row, pl.BlockSpec((1, V7X_SUBLANES, FFN_DIM), lambda s, i: (s, 0, 0))],
        out_shape=[jax.ShapeDtypeStruct((m, D_MODEL), F32),
                   jax.ShapeDtypeStruct((nb, V7X_SUBLANES, FFN_DIM), F32)],
        scratch_shapes=[pltpu.VMEM((tm, FFN_DIM), BF16)],
        compiler_params=_params("arbitrary", "arbitrary"),
        name="conv_ffn_seq",
    )(x, x, wup, wdn, cw, cb.reshape(1, -1), g.reshape(1, -1), b.reshape(1, -1))


def _ffn_state(x, p2, p1, wup, wdn, cw, cb, g, b):
    m = x.shape[0]
    full = lambda shape: pl.BlockSpec(shape, lambda i: (0,) * len(shape))
    return pl.pallas_call(
        _ffn_state_kernel,
        grid=(1,),
        in_specs=[full((m, D_MODEL)), full((m, FFN_DIM)), full((m, FFN_DIM)), _resident(wup.shape),
                  _resident(wdn.shape), _resident(cw.shape), _resident((1, FFN_DIM)),
                  full((1, D_MODEL)), full((1, D_MODEL))],
        out_specs=[full((m, D_MODEL)), full((m, FFN_DIM))],
        out_shape=[jax.ShapeDtypeStruct((m, D_MODEL), F32), jax.ShapeDtypeStruct((m, FFN_DIM), F32)],
        scratch_shapes=[pltpu.VMEM((m, FFN_DIM), BF16)],
        compiler_params=_params("arbitrary"),
        name="conv_ffn_state",
    )(x, p2, p1, wup, wdn, cw, cb.reshape(1, -1), g.reshape(1, -1), b.reshape(1, -1))


def _softmax_step(s, v, m_ref, l_ref, acc_ref):
    nl = s.shape[1] // V7X_LANES
    tiles = [s[:, j * V7X_LANES:(j + 1) * V7X_LANES] for j in range(nl)]
    m_prev = m_ref[...]
    m_new = jnp.maximum(m_prev, jnp.max(functools.reduce(jnp.maximum, tiles), axis=1, keepdims=True))
    alpha = jnp.exp(m_prev - m_new)
    ps = [jnp.exp(t - m_new) for t in tiles]
    l_ref[...] = alpha * l_ref[...] + functools.reduce(jnp.add, ps)
    p = jnp.concatenate(ps, axis=1).astype(BF16)
    acc_ref[...] = alpha * acc_ref[...] + jnp.dot(p, v, preferred_element_type=F32)
    m_ref[...] = m_new


def _diff_lambda(lam_ref, lam_init):
    lp = lam_ref[...]
    s1 = jnp.sum(lp[0:1, :] * lp[1:2, :], axis=1, keepdims=True)
    s2 = jnp.sum(lp[2:3, :] * lp[3:4, :], axis=1, keepdims=True)
    return jnp.exp(s1) - jnp.exp(s2) + lam_init


def _diff_finish(o, subln, lam_init):
    ms = jnp.mean(o * o, axis=-1, keepdims=True)
    return o * lax.rsqrt(ms + NORM_EPS) * subln * (1.0 - lam_init)


def _flash_sweep(qi, tq, step, diagonal):
    lax.fori_loop(0, qi // 2, lambda j, c: (step(j * (2 * tq), 2 * tq), c)[1], 0)

    @pl.when(qi % 2 == 1)
    def _():
        step((qi - 1) * tq, tq)

    diagonal(qi * tq)


def _stack_halves(x1, x2):
    half = x1.shape[0] // 2
    return jnp.concatenate([x1[0:half], x2[0:half], x1[half:], x2[half:]], axis=0)


def _unstack_halves(x):
    q = x.shape[0] // 4
    return (jnp.concatenate([x[0:q], x[2 * q:3 * q]], axis=0), jnp.concatenate([x[q:2 * q], x[3 * q:]], axis=0))


def _diagonal_chunk(koff, tq, scores, values, m_s, l_s, a_s):
    half = tq // 2
    tri = lax.broadcasted_iota(jnp.int32, (half, half), 1) <= lax.broadcasted_iota(jnp.int32, (half, half), 0)
    tri = jnp.where(tri, 1.0, 0.0)
    ones = jnp.ones((half, half), F32)
    keep_a = jnp.concatenate([tri, tri, ones, ones], axis=0) > 0.5
    s_a = jnp.where(keep_a, scores(slice(0, 2 * tq), koff, half), NEG_BIG)
    _softmax_step(s_a, values(koff, half), m_s, l_s, a_s)
    lo = slice(tq, 2 * tq)
    keep_b = jnp.concatenate([tri, tri], axis=0) > 0.5
    s_b = jnp.where(keep_b, scores(lo, koff + half, half), NEG_BIG)
    _softmax_step(s_b, values(koff + half, half), m_s.at[lo], l_s.at[lo], a_s.at[lo])


def _diff_attn_kernel(lam_ref, subln_ref, q_ref, k_ref, v_ref, o_ref, m_s, l_s, a_s, *, tq, lam_init):
    qi = pl.program_id(2)
    q = q_ref[0]
    lane = lax.broadcasted_iota(jnp.int32, q.shape, 1)
    zero = jnp.zeros_like(q)
    q12 = _stack_halves(jnp.where(lane < HEAD_DIM, q, zero), jnp.where(lane >= HEAD_DIM, q, zero))
    m_s[...] = jnp.full(m_s.shape, NEG_BIG, F32)
    l_s[...] = jnp.zeros(l_s.shape, F32)
    a_s[...] = jnp.zeros(a_s.shape, F32)
    nt_dims = (((1,), (1,)), ((), ()))
    half = tq // 2

    def scores(rows, koff, width):
        off = pl.multiple_of(koff, half)
        return lax.dot_general(q12[rows], k_ref[0, pl.ds(off, width), :], nt_dims, preferred_element_type=F32)

    def values(koff, width):
        return v_ref[0, pl.ds(pl.multiple_of(koff, half), width), :]

    def step(koff, width):
        _softmax_step(scores(slice(0, 2 * tq), koff, width), values(koff, width), m_s, l_s, a_s)

    _flash_sweep(qi, tq, step, lambda koff: _diagonal_chunk(koff, tq, scores, values, m_s, l_s, a_s))

    lam = _diff_lambda(lam_ref, lam_init)
    on1, on2 = _unstack_halves(a_s[...] / jnp.sum(l_s[...], axis=1, keepdims=True))
    o_ref[0] = _diff_finish(on1 - lam * on2, subln_ref[...], lam_init).astype(BF16)


def _diff_attn(q, k, v, lam_p, subln, lam_init, *, tq):
    nb, t, _ = q.shape
    qspec = pl.BlockSpec((1, tq, V7X_LANES), lambda b, h, i: (b, i, h))
    kvspec = pl.BlockSpec((1, t, V7X_LANES), lambda b, h, i: (b, 0, h))
    small = lambda shape: pl.BlockSpec(shape, lambda b, h, i: (0,) * len(shape))
    stat = pltpu.VMEM((2 * tq, V7X_LANES), F32)
    return pl.pallas_call(
        functools.partial(_diff_attn_kernel, tq=tq, lam_init=lam_init),
        grid=(nb, DIFF_HEADS, t // tq),
        in_specs=[small((4, HEAD_DIM)), small((1, 2 * HEAD_DIM)), qspec, kvspec, kvspec],
        out_specs=qspec,
        out_shape=jax.ShapeDtypeStruct((nb, t, D_MODEL), BF16),
        scratch_shapes=[stat, stat, stat],
        compiler_params=_params("arbitrary", "arbitrary", "arbitrary"),
        name="diff_attn",
    )(lam_p, subln.reshape(1, -1), q, k, v)


def _beats(row, g, tie_wins):
    return jnp.where(tie_wins, jnp.where(row >= g, 1.0, 0.0), jnp.where(row > g, 1.0, 0.0))


def _moba_select_bias(g_t, own, n_blocks):
    blk = lax.broadcasted_iota(jnp.int32, g_t.shape, 0)
    cnt = jnp.zeros(g_t.shape, F32)
    for n in range(n_blocks):
        cnt = cnt + _beats(g_t[n:n + 1, :], g_t, blk > n) * jnp.where(n < own, 1.0, 0.0)
    visible = jnp.where(cnt < MOBA_TOPK, 1.0, 0.0) * jnp.where(blk < own, 1.0, 0.0) + jnp.where(blk == own, 1.0, 0.0)
    return jnp.where(visible > 0.5, 0.0, NEG_BIG)


def _moba_attn_kernel(q_ref, k_ref, v_ref, mean_ref, o_ref, m_s, l_s, a_s, eye_s, *, tq, n_blocks):
    qi = pl.program_id(2)
    q = q_ref[0]
    lane = lax.broadcasted_iota(jnp.int32, q.shape, 1)
    zero = jnp.zeros_like(q)
    nt_dims = (((1,), (1,)), ((), ()))
    means = mean_ref[0].astype(BF16)
    own = (qi * tq + lax.broadcasted_iota(jnp.int32, (1, tq), 1)) // MOBA_BLOCK

    @pl.when(qi == 0)
    def _():
        same = lax.broadcasted_iota(jnp.int32, (tq, tq), 0) == lax.broadcasted_iota(jnp.int32, (tq, tq), 1)
        eye_s[...] = jnp.where(same, 1.0, 0.0).astype(BF16)

    eye = eye_s[...]
    pad = jnp.zeros((V7X_LANES - n_blocks, tq), BF16)
    qs, sels = [], []
    for qq in (jnp.where(lane < HEAD_DIM, q, zero), jnp.where(lane >= HEAD_DIM, q, zero)):
        g_t = lax.dot_general(means, qq, nt_dims, preferred_element_type=F32)
        bias_t = jnp.concatenate([_moba_select_bias(g_t, own, n_blocks).astype(BF16), pad], axis=0)
        sels.append(lax.dot_general(eye, bias_t, nt_dims, preferred_element_type=F32).astype(BF16))
        qs.append(qq)
    q_aug = jnp.concatenate([_stack_halves(*qs), _stack_halves(*sels)], axis=1)
    m_s[...] = jnp.full(m_s.shape, NEG_BIG, F32)
    l_s[...] = jnp.zeros(l_s.shape, F32)
    a_s[...] = jnp.zeros(a_s.shape, F32)
    half = tq // 2

    def scores(rows, koff, width):
        off = pl.multiple_of(koff, half)
        rel = (lax.broadcasted_iota(jnp.int32, (width, V7X_LANES), 1)
               - lax.broadcasted_iota(jnp.int32, (width, V7X_LANES), 0) // MOBA_BLOCK)
        onehot = jnp.where(rel == koff // MOBA_BLOCK, 1.0, 0.0).astype(BF16)
        k_aug = jnp.concatenate([k_ref[0, pl.ds(off, width), :], onehot], axis=1)
        return lax.dot_general(q_aug[rows], k_aug, nt_dims, preferred_element_type=F32)

    def values(koff, width):
        return v_ref[0, pl.ds(pl.multiple_of(koff, half), width), :]

    def step(koff, width):
        _softmax_step(scores(slice(0, 2 * tq), koff, width), values(koff, width), m_s, l_s, a_s)

    _flash_sweep(qi, tq, step, lambda koff: _diagonal_chunk(koff, tq, scores, values, m_s, l_s, a_s))
    on1, on2 = _unstack_halves(a_s[...] / jnp.sum(l_s[...], axis=1, keepdims=True))
    o_ref[0] = jnp.where(lane < HEAD_DIM, on1, on2).astype(BF16)


def _moba_attn(q, k, v, kmeans, *, tq):
    nb, t, _ = q.shape
    n_blocks = t // MOBA_BLOCK
    qspec = pl.BlockSpec((1, tq, V7X_LANES), lambda b, h, i: (b, i, h))
    kvspec = pl.BlockSpec((1, t, V7X_LANES), lambda b, h, i: (b, 0, h))
    mspec = pl.BlockSpec((1, n_blocks, V7X_LANES), lambda b, h, i: (b, 0, h))
    stat = pltpu.VMEM((2 * tq, V7X_LANES), F32)
    return pl.pallas_call(
        functools.partial(_moba_attn_kernel, tq=tq, n_blocks=n_blocks),
        grid=(nb, MOBA_HEADS // 2, t // tq),
        in_specs=[qspec, kvspec, kvspec, mspec],
        out_specs=qspec,
        out_shape=jax.ShapeDtypeStruct((nb, t, D_MODEL), BF16),
        scratch_shapes=[stat, stat, stat, pltpu.VMEM((tq, tq), BF16)],
        compiler_params=_params("arbitrary", "arbitrary", "arbitrary"),
        name="moba_attn",
    )(q, k, v, kmeans)


def _softplus(z):
    return jnp.maximum(z, 0.0) + jnp.log1p(jnp.exp(-jnp.abs(z)))


def _rg_gates(uc, wa_ref, ba_ref, wi_ref, bi_ref, lam_ref, a_ref, b_ref):
    sp = _softplus(-lam_ref[...])
    for n in range(RG_BLOCKS):
        sl = slice(n * RG_BW, (n + 1) * RG_BW)
        ub = uc[:, sl]
        ubb = ub.astype(BF16)
        r = jax.nn.sigmoid(jnp.dot(ubb, wa_ref[n], preferred_element_type=F32) + ba_ref[:, sl])
        ig = jax.nn.sigmoid(jnp.dot(ubb, wi_ref[n], preferred_element_type=F32) + bi_ref[:, sl])
        log_a = -RG_C * r * sp[:, sl]
        a = jnp.exp(log_a)
        a_ref[:, sl] = a
        b_ref[:, sl] = jnp.sqrt(1.0 - a * a) * (ig * ub)


def _rg_seq_kernel(x_ref, xh_ref, win_ref, cw_ref, cb_ref, wa_ref, ba_ref, wi_ref, bi_ref, lam_ref,
                   y_ref, hl_ref, tail_ref, gate_s, a_s, b_s, h_s, carry_s):
    i = pl.program_id(1)
    tm = x_ref.shape[0]
    w = RG_WIDTH
    xb = x_ref[...].astype(BF16)
    xh = jnp.where(i == 0, 0.0, xh_ref[...]).astype(BF16)
    gate_s[...] = jnp.dot(xb, win_ref[:, 0:w], preferred_element_type=F32)
    u = jnp.dot(xb, win_ref[:, w:2 * w], preferred_element_type=F32)
    uh = jnp.dot(xh, win_ref[:, w:2 * w], preferred_element_type=F32)
    cw = cw_ref[...]
    uc = cb_ref[...] + u * cw[3:4, :]
    for s in range(1, RG_CONV):
        uc = uc + _shift_rows(u, uh, s) * cw[3 - s:4 - s, :]
    _rg_gates(uc, wa_ref, ba_ref, wi_ref, bi_ref, lam_ref, a_s, b_s)

    @pl.when(i == 0)
    def _():
        carry_s[...] = jnp.zeros_like(carry_s)

    row = lax.broadcasted_iota(jnp.int32, (V7X_SUBLANES, w), 0)

    def group(gi, c):
        off = pl.multiple_of(gi * V7X_SUBLANES, V7X_SUBLANES)
        a8 = a_s[pl.ds(off, V7X_SUBLANES), :]
        b8 = b_s[pl.ds(off, V7X_SUBLANES), :]
        for s in (1, 2, 4):
            ok = row >= s
            b8 = jnp.where(ok, a8 * pltpu.roll(b8, s, axis=0) + b8, b8)
            a8 = jnp.where(ok, a8 * pltpu.roll(a8, s, axis=0), a8)
        h8 = a8 * carry_s[V7X_SUBLANES - 1:V7X_SUBLANES, :] + b8
        h_s[pl.ds(off, V7X_SUBLANES), :] = h8
        carry_s[...] = h8
        return c

    lax.fori_loop(0, tm // V7X_SUBLANES, group, 0)
    y_ref[...] = (jax.nn.gelu(gate_s[...], approximate=True) * h_s[...]).astype(BF16)

    @pl.when(i == pl.num_programs(1) - 1)
    def _():
        hl_ref[0] = carry_s[...]
        tail_ref[0] = u[tm - V7X_SUBLANES:tm, :]


def _rg_seq(x, win, cw, cb, wa, ba, wi, bi, lam, *, nb, tm):
    m = x.shape[0]
    nt = m // nb // tm
    hb = tm // V7X_SUBLANES
    w = RG_WIDTH
    row = pl.BlockSpec((tm, D_MODEL), lambda s, i: (s * nt + i, 0))
    halo = pl.BlockSpec((V7X_SUBLANES, D_MODEL), lambda s, i: (jnp.maximum((s * nt + i) * hb - 1, 0), 0))
    vec = _resident((1, w))
    per_seq = pl.BlockSpec((1, V7X_SUBLANES, w), lambda s, i: (s, 0, 0))
    big = pltpu.VMEM((tm, w), F32)
    return pl.pallas_call(
        _rg_seq_kernel,
        grid=(nb, nt),
        in_specs=[row, halo, _resident(win.shape), _resident(cw.shape), vec, _resident(wa.shape), vec,
                  _resident(wi.shape), vec, vec],
        out_specs=[pl.BlockSpec((tm, w), lambda s, i: (s * nt + i, 0)), per_seq, per_seq],
        out_shape=[jax.ShapeDtypeStruct((m, w), BF16), jax.ShapeDtypeStruct((nb, V7X_SUBLANES, w), F32),
                   jax.ShapeDtypeStruct((nb, V7X_SUBLANES, w), F32)],
        scratch_shapes=[big, big, big, big, pltpu.VMEM((V7X_SUBLANES, w), F32)],
        compiler_params=_params("arbitrary", "arbitrary"),
        name="rglru_seq",
    )(x, x, win, cw, cb.reshape(1, -1), wa, ba.reshape(1, -1), wi, bi.reshape(1, -1), lam.reshape(1, -1))


def _rg_state_kernel(x_ref, h0_ref, c0_ref, c1_ref, c2_ref, win_ref, cw_ref, cb_ref, wa_ref, ba_ref, wi_ref, bi_ref,
                     lam_ref, y_ref, h_ref, u_ref, a_s, b_s):
    w = RG_WIDTH
    xb = x_ref[...].astype(BF16)
    gate = jnp.dot(xb, win_ref[:, 0:w], preferred_element_type=F32)
    u = jnp.dot(xb, win_ref[:, w:2 * w], preferred_element_type=F32)
    cw = cw_ref[...]
    uc = cb_ref[...] + c0_ref[...] * cw[0:1, :] + c1_ref[...] * cw[1:2, :] + c2_ref[...] * cw[2:3, :] + u * cw[3:4, :]
    _rg_gates(uc, wa_ref, ba_ref, wi_ref, bi_ref, lam_ref, a_s, b_s)
    h = a_s[...] * h0_ref[...] + b_s[...]
    h_ref[...] = h
    u_ref[...] = u
    y_ref[...] = (jax.nn.gelu(gate, approximate=True) * h).astype(BF16)


def _rg_state(x, h0, c0, c1, c2, win, cw, cb, wa, ba, wi, bi, lam):
    m = x.shape[0]
    w = RG_WIDTH
    full = lambda shape: pl.BlockSpec(shape, lambda i: (0,) * len(shape))
    st = full((m, w))
    vec = full((1, w))
    return pl.pallas_call(
        _rg_state_kernel,
        grid=(1,),
        in_specs=[full((m, D_MODEL)), st, st, st, st, _resident(win.shape), full(cw.shape), vec,
                  _resident(wa.shape), vec, _resident(wi.shape), vec, vec],
        out_specs=[st, st, st],
        out_shape=[jax.ShapeDtypeStruct((m, w), BF16), jax.ShapeDtypeStruct((m, w), F32),
                   jax.ShapeDtypeStruct((m, w), F32)],
        scratch_shapes=[pltpu.VMEM((m, w), F32), pltpu.VMEM((m, w), F32)],
        compiler_params=_params("arbitrary"),
        name="rglru_state",
    )(x, h0, c0, c1, c2, win, cw, cb.reshape(1, -1), wa, ba.reshape(1, -1), wi, bi.reshape(1, -1), lam.reshape(1, -1))


def _page_scores(k_refs, q_b):
    return jnp.concatenate([jnp.sum(kr[0] * q_b, axis=1) for kr in k_refs], axis=1)


def _diff_decode_kernel(pt_ref, lam_ref, subln_ref, qb_ref, q_ref, kn_ref, vn_ref, *rest, n_pg, lam_init):
    del pt_ref
    k_refs, v_refs = rest[:n_pg], rest[n_pg:2 * n_pg]
    o_ref, m_s, l_s, a_s = rest[2 * n_pg:]
    step = pl.program_id(1)
    h = DIFF_HEADS

    @pl.when(step == 0)
    def _():
        s_new = jnp.sum(kn_ref[0] * q_ref[0], axis=-1, keepdims=True)
        m_s[...] = jnp.broadcast_to(s_new, m_s.shape)
        l_s[...] = jnp.full(l_s.shape, 1.0 / V7X_LANES, F32)
        a_s[...] = vn_ref[0]

    s = _page_scores(k_refs, qb_ref[0])
    m_prev = m_s[...]
    m_new = jnp.maximum(m_prev, jnp.max(s, axis=1, keepdims=True))
    alpha = jnp.exp(m_prev - m_new)
    p = jnp.exp(s - jnp.concatenate([m_new] * n_pg, axis=1))
    l_s[...] = alpha * l_s[...] + functools.reduce(
        jnp.add, [p[:, g * V7X_LANES:(g + 1) * V7X_LANES] for g in range(n_pg)])
    pb = p.astype(BF16)
    wide = PAGE_SIZE * h
    col = lax.broadcasted_iota(jnp.int32, (PAGE_SIZE, wide), 1)
    spread = jnp.where(col // h == lax.broadcasted_iota(jnp.int32, (PAGE_SIZE, wide), 0), 1.0, 0.0).astype(BF16)
    mine = (lax.broadcasted_iota(jnp.int32, (2 * h, wide), 1) % h) == (lax.broadcasted_iota(jnp.int32, (2 * h, wide), 0) // 2)
    pv = jnp.zeros(a_s.shape, F32)
    for g, vr in enumerate(v_refs):
        p_wide = jnp.dot(pb[:, g * V7X_LANES:(g + 1) * V7X_LANES], spread, preferred_element_type=F32)
        p_wide = jnp.where(mine, p_wide, 0.0).astype(BF16)
        v2 = vr[0].reshape(wide, 2 * HEAD_DIM).astype(BF16)
        pv = pv + jnp.dot(p_wide, v2, preferred_element_type=F32)
    a_s[...] = alpha * a_s[...] + pv
    m_s[...] = m_new

    @pl.when(step == pl.num_programs(1) - 1)
    def _():
        lam = _diff_lambda(lam_ref, lam_init)
        a_s[...] = a_s[...] / jnp.sum(l_s[...], axis=1, keepdims=True)
        o = a_s[pl.ds(0, h, stride=2), :] - lam * a_s[pl.ds(1, h, stride=2), :]
        o_ref[0] = _diff_finish(o, subln_ref[...], lam_init).astype(BF16)


def _decode_specs(page_table, n_pg, kshape, vshape):
    per_seq = lambda shape: pl.BlockSpec((1,) + shape, lambda b, s, pt: (b,) + (0,) * len(shape))

    def page(shape, g):
        return pl.BlockSpec((1,) + shape, lambda b, s, pt: (pt[b, s * n_pg + g],) + (0,) * len(shape))

    return per_seq, [page(kshape, g) for g in range(n_pg)] + [page(vshape, g) for g in range(n_pg)]


def _diff_decode(q, k_new, v_new, cache_k, cache_v, page_table, lam_p, subln, lam_init):
    nb, n_pages = page_table.shape
    n_pg = math.gcd(DECODE_PAGES, n_pages)
    h, c = DIFF_HEADS, 2 * DIFF_HEADS
    k_t = jnp.transpose(cache_k, (0, 2, 3, 1))
    q_b = jnp.broadcast_to(q[..., None], (nb, c, HEAD_DIM, PAGE_SIZE))
    v_rows = jnp.repeat(v_new, 2, axis=1)
    per_seq, pages = _decode_specs(page_table, n_pg, (c, HEAD_DIM, PAGE_SIZE), (PAGE_SIZE, h, 2 * HEAD_DIM))
    small = lambda shape: pl.BlockSpec(shape, lambda b, s, pt: (0,) * len(shape))
    stat = pltpu.VMEM((c, V7X_LANES), F32)
    return pl.pallas_call(
        functools.partial(_diff_decode_kernel, n_pg=n_pg, lam_init=lam_init),
        grid_spec=pltpu.PrefetchScalarGridSpec(
            num_scalar_prefetch=1,
            grid=(nb, n_pages // n_pg),
            in_specs=[small((4, HEAD_DIM)), small((1, 2 * HEAD_DIM)), per_seq((c, HEAD_DIM, PAGE_SIZE)),
                      per_seq((c, HEAD_DIM)), per_seq((c, HEAD_DIM)), per_seq((c, 2 * HEAD_DIM))] + pages,
            out_specs=per_seq((h, 2 * HEAD_DIM)),
            scratch_shapes=[stat, stat, stat],
        ),
        out_shape=jax.ShapeDtypeStruct((nb, h, 2 * HEAD_DIM), BF16),
        compiler_params=_params("arbitrary", "arbitrary"),
        name="diff_decode",
    )(page_table, lam_p, subln.reshape(1, -1), q_b, q, k_new, v_rows, *([k_t] * n_pg), *([cache_v] * n_pg))


def _moba_decode_kernel(pt_ref, qb_ref, q_ref, kn_ref, vn_ref, *rest, n_pg, n_pages):
    del pt_ref
    k_refs, v_refs = rest[:n_pg], rest[n_pg:2 * n_pg]
    o_ref, s_s, m_s, l_s, r_s = rest[2 * n_pg:]
    step = pl.program_id(1)
    c = MOBA_HEADS
    nt_dims = (((1,), (1,)), ((), ()))
    s = _page_scores(k_refs, qb_ref[0])
    for g in range(n_pg):
        sg = s[:, g * V7X_LANES:(g + 1) * V7X_LANES]
        mg = jnp.broadcast_to(jnp.max(sg, axis=1, keepdims=True), sg.shape)
        pg = jnp.exp(sg - mg)
        page = step * n_pg + g
        s_s[page] = sg
        m_s[page] = mg
        l_s[page] = pg
        v2 = v_refs[g][0].reshape(c * HEAD_DIM, PAGE_SIZE).astype(BF16)
        r_s[page] = lax.dot_general(pg.astype(BF16), v2, nt_dims, preferred_element_type=F32)

    @pl.when(step == pl.num_programs(1) - 1)
    def _():
        pages_per_block = MOBA_BLOCK // PAGE_SIZE
        nblk = n_pages // pages_per_block
        wide = (n_pages, c, V7X_LANES)
        gsum = jnp.sum(s_s[...], axis=2, keepdims=True).reshape(nblk, pages_per_block, c, 1)
        gate = jnp.broadcast_to(jnp.sum(gsum, axis=1), (nblk, c, V7X_LANES))
        cnt = jnp.zeros(gate.shape, F32)
        blk = lax.broadcasted_iota(jnp.int32, gate.shape, 0)
        for n in range(nblk):
            cnt = cnt + _beats(gate[n:n + 1], gate, blk > n)
        selw = jnp.where(cnt < MOBA_TOPK, 1.0, 0.0)
        selw = jnp.broadcast_to(selw[:, None], (nblk, pages_per_block, c, V7X_LANES)).reshape(wide)
        sel = selw > 0.5
        s_new = jnp.broadcast_to(jnp.sum(kn_ref[0] * q_ref[0], axis=-1, keepdims=True), (c, V7X_LANES))
        mp = jnp.where(sel, m_s[...], NEG_BIG)
        m_all = jnp.maximum(jnp.max(mp, axis=0), s_new)
        w = jnp.where(sel, jnp.exp(mp - m_all[None]), 0.0)
        w_new = jnp.exp(s_new - m_all)
        lsum = jnp.sum(l_s[...], axis=2, keepdims=True)
        den = jnp.sum(w * lsum, axis=0) + w_new
        reps = c * HEAD_DIM // V7X_LANES
        num = jnp.sum(jnp.concatenate([w] * reps, axis=2) * r_s[...], axis=0)
        num = num + jnp.concatenate([w_new] * reps, axis=1) * vn_ref[0]
        row = lax.broadcasted_iota(jnp.int32, (c, HEAD_DIM), 0)
        out = jnp.zeros((c, HEAD_DIM), F32)
        for cc in range(c):
            out = jnp.where(row == cc, num[:, cc * HEAD_DIM:(cc + 1) * HEAD_DIM], out)
        o_ref[0] = (out / den[:, 0:HEAD_DIM]).astype(BF16)


def _moba_decode(q, k_new, v_new, cache_k, cache_v, page_table):
    nb, n_pages = page_table.shape
    n_pg = math.gcd(DECODE_PAGES, n_pages)
    c = MOBA_HEADS
    k_t = jnp.transpose(cache_k, (0, 2, 3, 1))
    v_t = jnp.transpose(cache_v, (0, 2, 3, 1))
    q_b = jnp.broadcast_to(q[..., None], (nb, c, HEAD_DIM, PAGE_SIZE))
    v_wide = jnp.tile(v_new, (1, 1, c))
    pshape = (c, HEAD_DIM, PAGE_SIZE)
    per_seq, pages = _decode_specs(page_table, n_pg, pshape, pshape)
    part = pltpu.VMEM((n_pages, c, V7X_LANES), F32)
    return pl.pallas_call(
        functools.partial(_moba_decode_kernel, n_pg=n_pg, n_pages=n_pages),
        grid_spec=pltpu.PrefetchScalarGridSpec(
            num_scalar_prefetch=1,
            grid=(nb, n_pages // n_pg),
            in_specs=[per_seq(pshape), per_seq((c, HEAD_DIM)), per_seq((c, HEAD_DIM)),
                      per_seq((c, c * HEAD_DIM))] + pages,
            out_specs=per_seq((c, HEAD_DIM)),
            scratch_shapes=[part, part, part, pltpu.VMEM((n_pages, c, c * HEAD_DIM), F32)],
        ),
        out_shape=jax.ShapeDtypeStruct((nb, c, HEAD_DIM), BF16),
        compiler_params=_params("arbitrary", "arbitrary"),
        name="moba_decode",
    )(page_table, q_b, q, k_new, v_wide, *([k_t] * n_pg), *([v_t] * n_pg))


def kernel(x_prompt, x_sample, cache_k_l0, cache_v_l0, cache_k_l1, cache_v_l1, cache_k_l3, cache_v_l3, state_rg_h_l2, state_rg_conv_l2, state_ffn_conv, page_table, l0_w_qkv, l0_lambda, l0_subln, l0_w_o, l1_w_qkv, l1_w_o, l2_w_in, l2_conv_w, l2_conv_b, l2_w_a, l2_b_a, l2_w_i, l2_b_i, l2_lambda, l2_w_o, l3_w_qkv, l3_lambda, l3_subln, l3_w_o, ffn_w_up, ffn_conv_w, ffn_conv_b, ffn_w_down, ln_mix_g, ln_mix_b, ln_ffn_g, ln_ffn_b):
    nb, t, d = x_prompt.shape
    ns = x_sample.shape[0]
    past_len = page_table.shape[1] * PAGE_SIZE
    tm = min(ROW_TILE, t)
    tab_p = _rope_tables(jnp.arange(t, dtype=jnp.int32))
    tab_s = _rope_tables(jnp.full((ns,), past_len, dtype=jnp.int32))
    xp = x_prompt.reshape(nb * t, d)
    xs = x_sample.reshape(ns, d)
    outs_p, outs_s, ffn_p, ffn_s = {}, {}, [], []

    def attn_layer(i, w_qkv, w_o, cache_k, cache_v, lam_p=None, subln=None):
        nonlocal xp, xs
        wq = w_qkv.astype(BF16)
        wo = w_o.astype(BF16)
        diff = lam_p is not None
        res_p = _qkv_rope(xp, wq, tab_p, tm=tm, rows_per_seq=t, with_means=not diff, k_t=True, v_t=not diff)
        q_p, kb_p, vb_p, k_p, v_p = res_p[:5]
        q_s, _, _, k_s, v_s = _qkv_rope(xs, wq, tab_s, tm=ns, rows_per_seq=ns, with_means=False)
        shp = lambda a: a.reshape(nb, t, d)
        q_s3 = q_s.astype(F32).reshape(ns, 2 * DIFF_HEADS, HEAD_DIM)
        k_s3 = k_s.reshape(ns, 2 * DIFF_HEADS, HEAD_DIM)
        if diff:
            lam_init = 0.8 - 0.6 * math.exp(-0.3 * i)
            o_p = _diff_attn(shp(q_p), shp(kb_p), shp(vb_p), lam_p, subln, lam_init, tq=min(ATT_TQ, t))
            v_s3 = v_s.reshape(ns, DIFF_HEADS, 2 * HEAD_DIM)
            o_s = _diff_decode(q_s3, k_s3, v_s3, cache_k, cache_v, page_table, lam_p, subln, lam_init)
            kv_shape = ((2 * DIFF_HEADS, HEAD_DIM), (DIFF_HEADS, 2 * HEAD_DIM))
        else:
            kmeans = res_p[5].reshape(nb, t // MOBA_BLOCK, d)
            o_p = _moba_attn(shp(q_p), shp(kb_p), shp(vb_p), kmeans, tq=min(ATT_TQ, t))
            v_s3 = v_s.reshape(ns, MOBA_HEADS, HEAD_DIM)
            o_s = _moba_decode(q_s3, k_s3, v_s3, cache_k, cache_v, page_table)
            kv_shape = ((MOBA_HEADS, HEAD_DIM), (MOBA_HEADS, HEAD_DIM))
        back = lambda a: jnp.transpose(a, (0, 3, 1, 2))
        outs_p[i] = (back(k_p), v_p.reshape(nb, t, *kv_shape[1]) if diff else back(v_p))
        outs_s[i] = (k_s.reshape(ns, 1, *kv_shape[0]), v_s.reshape(ns, 1, *kv_shape[1]))
        xp = _proj_ln(o_p.reshape(nb * t, d), xp, wo, ln_mix_g[i], ln_mix_b[i], tm=tm)
        xs = _proj_ln(o_s.reshape(ns, d), xs, wo, ln_mix_g[i], ln_mix_b[i], tm=ns)

    def rg_layer(i):
        nonlocal xp, xs
        win = l2_w_in.astype(BF16)
        wa = l2_w_a.astype(BF16)
        wi = l2_w_i.astype(BF16)
        wo = l2_w_o.astype(BF16)
        y_p, hl_p, tail_p = _rg_seq(xp, win, l2_conv_w, l2_conv_b, wa, l2_b_a, wi, l2_b_i, l2_lambda, nb=nb, tm=tm)
        cs = state_rg_conv_l2
        y_s, h_s, u_s = _rg_state(xs, state_rg_h_l2, cs[:, 0], cs[:, 1], cs[:, 2], win, l2_conv_w, l2_conv_b,
                                  wa, l2_b_a, wi, l2_b_i, l2_lambda)
        outs_p[i] = (hl_p[:, V7X_SUBLANES - 1], tail_p[:, V7X_SUBLANES - (RG_CONV - 1):])
        outs_s[i] = (h_s, jnp.concatenate([cs[:, 1:], u_s[:, None]], axis=1))
        xp = _proj_ln(y_p, xp, wo, ln_mix_g[i], ln_mix_b[i], tm=tm)
        xs = _proj_ln(y_s, xs, wo, ln_mix_g[i], ln_mix_b[i], tm=ns)

    def ffn_layer(i):
        nonlocal xp, xs
        wup = ffn_w_up[i].astype(BF16)
        wdn = ffn_w_down[i].astype(BF16)
        xp, tail = _ffn_seq(xp, wup, wdn, ffn_conv_w[i], ffn_conv_b[i], ln_ffn_g[i], ln_ffn_b[i], nb=nb,
                            tm=min(FFN_ROW_TILE, t))
        st = state_ffn_conv[i]
        xs, a_s = _ffn_state(xs, st[:, 0], st[:, 1], wup, wdn, ffn_conv_w[i], ffn_conv_b[i], ln_ffn_g[i], ln_ffn_b[i])
        ffn_p.append(tail[:, V7X_SUBLANES - (FFN_CONV - 1):])
        ffn_s.append(jnp.concatenate([st[:, 1:], a_s[:, None]], axis=1))

    attn_layer(0, l0_w_qkv, l0_w_o, cache_k_l0, cache_v_l0, l0_lambda, l0_subln)
    ffn_layer(0)
    attn_layer(1, l1_w_qkv, l1_w_o, cache_k_l1, cache_v_l1)
    ffn_layer(1)
    rg_layer(2)
    ffn_layer(2)
    attn_layer(3, l3_w_qkv, l3_w_o, cache_k_l3, cache_v_l3, l3_lambda, l3_subln)
    ffn_layer(3)

    return (xp.reshape(nb, t, d), xs.reshape(ns, 1, d),
            outs_p[0][0], outs_p[0][1], outs_p[1][0], outs_p[1][1], outs_p[2][0], outs_p[2][1],
            outs_p[3][0], outs_p[3][1], jnp.stack(ffn_p),
            outs_s[0][0], outs_s[0][1], outs_s[1][0], outs_s[1][1], outs_s[2][0], outs_s[2][1],
            outs_s[3][0], outs_s[3][1], jnp.stack(ffn_s))
```

```python
import functools
import math

import jax
import jax.numpy as jnp
from jax import lax
from jax.experimental import pallas as pl
from jax.experimental.pallas import tpu as pltpu

F32 = jnp.float32
BF16 = jnp.bfloat16

D_MODEL = 1024
DEPTH = 4
PAGE_SIZE = 128
HEAD_DIM = 64
DIFF_HEADS = D_MODEL // (2 * HEAD_DIM)
MOBA_HEADS = D_MODEL // HEAD_DIM
MOBA_BLOCK = 256
MOBA_TOPK = 3
RG_WIDTH = 3 * D_MODEL // 2
RG_BLOCKS = 12
RG_BW = RG_WIDTH // RG_BLOCKS
RG_CONV = 4
RG_C = 8.0
FFN_DIM = 2816
FFN_CONV = 3
ROPE_THETA = 10000.0
NORM_EPS = 1e-5
DEEPNORM_ALPHA = (2.0 * DEPTH) ** 0.25

V7X_LANES = 128
V7X_SUBLANES = 8
V7X_VMEM_BYTES = 64 * 1024 * 1024
VMEM_LIMIT = V7X_VMEM_BYTES * 7 // 8

NEG_BIG = -1e30

ROW_TILE = 512
FFN_ROW_TILE = 1024
FFN_CHUNK = 512
ATT_TQ = 512
DECODE_PAGES = 16


def _params(*sem):
    return pltpu.CompilerParams(dimension_semantics=sem, vmem_limit_bytes=VMEM_LIMIT)


def _resident(shape):
    nd = len(shape)
    return pl.BlockSpec(shape, lambda *_: (0,) * nd, pipeline_mode=pl.Buffered(1))


def _layer_norm(z, g, b):
    mu = jnp.mean(z, axis=-1, keepdims=True)
    zc = z - mu
    var = jnp.mean(zc * zc, axis=-1, keepdims=True)
    return zc * lax.rsqrt(var + NORM_EPS) * g + b


def _shift_rows(a, halo, s):
    row = lax.broadcasted_iota(jnp.int32, a.shape, 0)
    out = pltpu.roll(a, s, axis=0)
    hn = halo.shape[0]
    for r in range(s):
        out = jnp.where(row == r, halo[hn - s + r:hn - s + r + 1, :], out)
    return out


def _rope_cols(t, cos, sa, sb):
    return t * cos + pltpu.roll(t, HEAD_DIM // 2, axis=1) * sa + pltpu.roll(t, V7X_LANES - HEAD_DIM // 2, axis=1) * sb


def _store_heads(out_ref, cols, x, transposed):
    if transposed:
        pair = V7X_LANES // HEAD_DIM
        first = cols.start // HEAD_DIM
        out_ref[0, first:first + pair, :, :] = x.T.reshape(pair, HEAD_DIM, x.shape[0])
    else:
        out_ref[:, cols] = x


def _qkv_rope_kernel(x_ref, w_ref, cos_ref, sa_ref, sb_ref, q_ref, kb_ref, vb_ref, k_ref, v_ref, *rest,
                     q_scale, n_mean_blocks, k_t, v_t):
    xb = x_ref[...].astype(BF16)
    cos, sa, sb = cos_ref[...], sa_ref[...], sb_ref[...]
    d = D_MODEL
    q = jnp.dot(xb, w_ref[:, 0:d], preferred_element_type=F32)
    k = jnp.dot(xb, w_ref[:, d:2 * d], preferred_element_type=F32)
    v = jnp.dot(xb, w_ref[:, 2 * d:3 * d], preferred_element_type=F32)
    vb_ref[...] = v.astype(BF16)
    sums = [[] for _ in range(n_mean_blocks)]
    for c in range(d // V7X_LANES):
        sl = slice(c * V7X_LANES, (c + 1) * V7X_LANES)
        qc = _rope_cols(q[:, sl], cos, sa, sb)
        kc = _rope_cols(k[:, sl], cos, sa, sb)
        q_ref[:, sl] = (qc * q_scale).astype(BF16)
        kb_ref[:, sl] = kc.astype(BF16)
        _store_heads(k_ref, sl, kc, k_t)
        _store_heads(v_ref, sl, v[:, sl], v_t)
        for r in range(n_mean_blocks):
            sums[r].append(jnp.mean(kc[r * MOBA_BLOCK:(r + 1) * MOBA_BLOCK, :], axis=0, keepdims=True))
    for r in range(n_mean_blocks):
        rest[0][r] = jnp.concatenate(sums[r], axis=1)


def _rope_tables(pos):
    half = HEAD_DIM // 2
    inv_freq = ROPE_THETA ** (-jnp.arange(half, dtype=F32) / half)
    ang = pos.astype(F32)[:, None] * inv_freq[None, :]
    cos = jnp.cos(ang)
    sin = jnp.sin(ang)
    zero = jnp.zeros_like(sin)
    cos128 = jnp.tile(cos, (1, 4))
    sa128 = jnp.tile(jnp.concatenate([zero, sin], axis=1), (1, 2))
    sb128 = jnp.tile(jnp.concatenate([-sin, zero], axis=1), (1, 2))
    return cos128, sa128, sb128


def _qkv_rope(x, w, tables, *, tm, rows_per_seq, with_means, k_t=False, v_t=False):
    m = x.shape[0]
    nt = rows_per_seq // tm
    n_mean = tm // MOBA_BLOCK if with_means else 0
    row = pl.BlockSpec((tm, D_MODEL), lambda i: (i, 0))
    tab = pl.BlockSpec((tm, V7X_LANES), lambda i: (i % nt, 0))
    heads = D_MODEL // HEAD_DIM
    row_f32 = jax.ShapeDtypeStruct((m, D_MODEL), F32)
    t_f32 = jax.ShapeDtypeStruct((m // rows_per_seq, heads, HEAD_DIM, rows_per_seq), F32)
    t_spec = pl.BlockSpec((1, heads, HEAD_DIM, tm), lambda i: (i // nt, 0, 0, i % nt))
    bf = jax.ShapeDtypeStruct((m, D_MODEL), BF16)
    out_shape = [bf, bf, bf, t_f32 if k_t else row_f32, t_f32 if v_t else row_f32]
    out_specs = [row, row, row, t_spec if k_t else row, t_spec if v_t else row]
    if with_means:
        out_shape.append(jax.ShapeDtypeStruct((m // MOBA_BLOCK, 1, D_MODEL), F32))
        out_specs.append(pl.BlockSpec((n_mean, 1, D_MODEL), lambda i: (i, 0, 0)))
    return pl.pallas_call(
        functools.partial(_qkv_rope_kernel, q_scale=HEAD_DIM ** -0.5, n_mean_blocks=n_mean, k_t=k_t, v_t=v_t),
        grid=(m // tm,),
        in_specs=[row, _resident(w.shape), tab, tab, tab],
        out_specs=out_specs,
        out_shape=out_shape,
        compiler_params=_params("arbitrary"),
        name="qkv_rope",
    )(x, w, *tables)


def _proj_ln_kernel(o_ref, x_ref, w_ref, g_ref, b_ref, out_ref):
    y = jnp.dot(o_ref[...], w_ref[...], preferred_element_type=F32)
    out_ref[...] = _layer_norm(DEEPNORM_ALPHA * x_ref[...] + y, g_ref[...], b_ref[...])


def _proj_ln(o, x, w, g, b, *, tm):
    m, kd = o.shape
    row_o = pl.BlockSpec((tm, kd), lambda i: (i, 0))
    row_x = pl.BlockSpec((tm, D_MODEL), lambda i: (i, 0))
    vec = pl.BlockSpec((1, D_MODEL), lambda i: (0, 0))
    return pl.pallas_call(
        _proj_ln_kernel,
        grid=(m // tm,),
        in_specs=[row_o, row_x, _resident(w.shape), vec, vec],
        out_specs=row_x,
        out_shape=jax.ShapeDtypeStruct((m, D_MODEL), F32),
        compiler_params=_params("arbitrary"),
        name="proj_ln",
    )(o, x, w, g.reshape(1, -1), b.reshape(1, -1))


def _ffn_chunks(xb, xh, p_refs, wup_ref, wdn_ref, cw_ref, cb_ref, acc_ref, a_out_ref, write_tail):
    tm = xb.shape[0]

    def chunk(off, width):
        wa = wup_ref[:, off:off + width]
        wg = wup_ref[:, FFN_DIM + off:FFN_DIM + off + width]
        a = jnp.dot(xb, wa, preferred_element_type=F32)
        g = jnp.dot(xb, wg, preferred_element_type=F32)
        if xh is not None:
            ah = jnp.dot(xh, wa, preferred_element_type=F32)
            p1 = _shift_rows(a, ah, 1)
            p2 = _shift_rows(a, ah, 2)
        else:
            p2 = p_refs[0][:, off:off + width]
            p1 = p_refs[1][:, off:off + width]
        cw = cw_ref[:, off:off + width]
        cb = cb_ref[:, off:off + width]
        ac = cb + p2 * cw[0:1, :] + p1 * cw[1:2, :] + a * cw[2:3, :]
        acc_ref[:, off:off + width] = (jax.nn.gelu(ac, approximate=True) * g).astype(BF16)
        if write_tail is None:
            a_out_ref[:, off:off + width] = a
        else:
            @pl.when(write_tail)
            def _():
                a_out_ref[0, :, off:off + width] = a[tm - V7X_SUBLANES:tm, :]

    for off in range(0, FFN_DIM, FFN_CHUNK):
        chunk(off, min(FFN_CHUNK, FFN_DIM - off))
    return jnp.dot(acc_ref[...], wdn_ref[...], preferred_element_type=F32)


def _ffn_seq_kernel(x_ref, xh_ref, wup_ref, wdn_ref, cw_ref, cb_ref, g_ref, b_ref, out_ref, tail_ref, acc_ref):
    i = pl.program_id(1)
    x = x_ref[...]
    xb = x.astype(BF16)
    xh = jnp.where(i == 0, 0.0, xh_ref[...]).astype(BF16)
    y = _ffn_chunks(xb, xh, None, wup_ref, wdn_ref, cw_ref, cb_ref, acc_ref, tail_ref, i == pl.num_programs(1) - 1)
    out_ref[...] = _layer_norm(DEEPNORM_ALPHA * x + y, g_ref[...], b_ref[...])


def _ffn_state_kernel(x_ref, p2_ref, p1_ref, wup_ref, wdn_ref, cw_ref, cb_ref, g_ref, b_ref, out_ref, a_ref, acc_ref):
    x = x_ref[...]
    y = _ffn_chunks(x.astype(BF16), None, (p2_ref, p1_ref), wup_ref, wdn_ref, cw_ref, cb_ref, acc_ref, a_ref, None)
    out_ref[...] = _layer_norm(DEEPNORM_ALPHA * x + y, g_ref[...], b_ref[...])


def _ffn_seq(x, wup, wdn, cw, cb, g, b, *, nb, tm):
    m = x.shape[0]
    nt = m // nb // tm
    hb = tm // V7X_SUBLANES
    row = pl.BlockSpec((tm, D_MODEL), lambda s, i: (s * nt + i, 0))
    halo = pl.BlockSpec((V7X_SUBLANES, D_MODEL), lambda s, i: (jnp.maximum((s * nt + i) * hb - 1, 0), 0))
    vec_d = pl.BlockSpec((1, D_MODEL), lambda s, i: (0, 0))
    return pl.pallas_call(
        _ffn_seq_kernel,
        grid=(nb, nt),
        in_specs=[row, halo, _resident(wup.shape), _resident(wdn.shape), _resident(cw.shape),
                  _resident((1, FFN_DIM)), vec_d, vec_d],
        out_specs=[row, pl.BlockSpec((1, V7X_SUBLANES, FFN_DIM), lambda s, i: (s, 0, 0))],
        out_shape=[jax.ShapeDtypeStruct((m, D_MODEL), F32),
                   jax.ShapeDtypeStruct((nb, V7X_SUBLANES, FFN_DIM), F32)],
        scratch_shapes=[pltpu.VMEM((tm, FFN_DIM), BF16)],
        compiler_params=_params("arbitrary", "arbitrary"),
        name="conv_ffn_seq",
    )(x, x, wup, wdn, cw, cb.reshape(1, -1), g.reshape(1, -1), b.reshape(1, -1))


def _ffn_state(x, p2, p1, wup, wdn, cw, cb, g, b):
    m = x.shape[0]
    full = lambda shape: pl.BlockSpec(shape, lambda i: (0,) * len(shape))
    return pl.pallas_call(
        _ffn_state_kernel,
        grid=(1,),
        in_specs=[full((m, D_MODEL)), full((m, FFN_DIM)), full((m, FFN_DIM)), _resident(wup.shape),
                  _resident(wdn.shape), _resident(cw.shape), _resident((1, FFN_DIM)),
                  full((1, D_MODEL)), full((1, D_MODEL))],
        out_specs=[full((m, D_MODEL)), full((m, FFN_DIM))],
        out_shape=[jax.ShapeDtypeStruct((m, D_MODEL), F32), jax.ShapeDtypeStruct((m, FFN_DIM), F32)],
        scratch_shapes=[pltpu.VMEM((m, FFN_DIM), BF16)],
        compiler_params=_params("arbitrary"),
        name="conv_ffn_state",
    )(x, p2, p1, wup, wdn, cw, cb.reshape(1, -1), g.reshape(1, -1), b.reshape(1, -1))


def _softmax_step(s, v, m_ref, l_ref, acc_ref):
    nl = s.shape[1] // V7X_LANES
    tiles = [s[:, j * V7X_LANES:(j + 1) * V7X_LANES] for j in range(nl)]
    m_prev = m_ref[...]
    m_new = jnp.maximum(m_prev, jnp.max(functools.reduce(jnp.maximum, tiles), axis=1, keepdims=True))
    alpha = jnp.exp(m_prev - m_new)
    ps = [jnp.exp(t - m_new) for t in tiles]
    l_ref[...] = alpha * l_ref[...] + functools.reduce(jnp.add, ps)
    p = jnp.concatenate(ps, axis=1).astype(BF16)
    acc_ref[...] = alpha * acc_ref[...] + jnp.dot(p, v, preferred_element_type=F32)
    m_ref[...] = m_new


def _diff_lambda(lam_ref, lam_init):
    lp = lam_ref[...]
    s1 = jnp.sum(lp[0:1, :] * lp[1:2, :], axis=1, keepdims=True)
    s2 = jnp.sum(lp[2:3, :] * lp[3:4, :], axis=1, keepdims=True)
    return jnp.exp(s1) - jnp.exp(s2) + lam_init


def _diff_finish(o, subln, lam_init):
    ms = jnp.mean(o * o, axis=-1, keepdims=True)
    return o * lax.rsqrt(ms + NORM_EPS) * subln * (1.0 - lam_init)


def _flash_sweep(qi, tq, step, diagonal):
    lax.fori_loop(0, qi // 2, lambda j, c: (step(j * (2 * tq), 2 * tq), c)[1], 0)

    @pl.when(qi % 2 == 1)
    def _():
        step((qi - 1) * tq, tq)

    diagonal(qi * tq)


def _stack_halves(x1, x2):
    half = x1.shape[0] // 2
    return jnp.concatenate([x1[0:half], x2[0:half], x1[half:], x2[half:]], axis=0)


def _unstack_halves(x):
    q = x.shape[0] // 4
    return (jnp.concatenate([x[0:q], x[2 * q:3 * q]], axis=0), jnp.concatenate([x[q:2 * q], x[3 * q:]], axis=0))


def _diagonal_chunk(koff, tq, scores, values, m_s, l_s, a_s):
    half = tq // 2
    tri = lax.broadcasted_iota(jnp.int32, (half, half), 1) <= lax.broadcasted_iota(jnp.int32, (half, half), 0)
    tri = jnp.where(tri, 1.0, 0.0)
    ones = jnp.ones((half, half), F32)
    keep_a = jnp.concatenate([tri, tri, ones, ones], axis=0) > 0.5
    s_a = jnp.where(keep_a, scores(slice(0, 2 * tq), koff, half), NEG_BIG)
    _softmax_step(s_a, values(koff, half), m_s, l_s, a_s)
    lo = slice(tq, 2 * tq)
    keep_b = jnp.concatenate([tri, tri], axis=0) > 0.5
    s_b = jnp.where(keep_b, scores(lo, koff + half, half), NEG_BIG)
    _softmax_step(s_b, values(koff + half, half), m_s.at[lo], l_s.at[lo], a_s.at[lo])


def _diff_attn_kernel(lam_ref, subln_ref, q_ref, k_ref, v_ref, o_ref, m_s, l_s, a_s, *, tq, lam_init):
    qi = pl.program_id(2)
    q = q_ref[0]
    lane = lax.broadcasted_iota(jnp.int32, q.shape, 1)
    zero = jnp.zeros_like(q)
    q12 = _stack_halves(jnp.where(lane < HEAD_DIM, q, zero), jnp.where(lane >= HEAD_DIM, q, zero))
    m_s[...] = jnp.full(m_s.shape, NEG_BIG, F32)
    l_s[...] = jnp.zeros(l_s.shape, F32)
    a_s[...] = jnp.zeros(a_s.shape, F32)
    nt_dims = (((1,), (1,)), ((), ()))
    half = tq // 2

    def scores(rows, koff, width):
        off = pl.multiple_of(koff, half)
        return lax.dot_general(q12[rows], k_ref[0, pl.ds(off, width), :], nt_dims, preferred_element_type=F32)

    def values(koff, width):
        return v_ref[0, pl.ds(pl.multiple_of(koff, half), width), :]

    def step(koff, width):
        _softmax_step(scores(slice(0, 2 * tq), koff, width), values(koff, width), m_s, l_s, a_s)

    _flash_sweep(qi, tq, step, lambda koff: _diagonal_chunk(koff, tq, scores, values, m_s, l_s, a_s))

    lam = _diff_lambda(lam_ref, lam_init)
    on1, on2 = _unstack_halves(a_s[...] / jnp.sum(l_s[...], axis=1, keepdims=True))
    o_ref[0] = _diff_finish(on1 - lam * on2, subln_ref[...], lam_init).astype(BF16)


def _diff_attn(q, k, v, lam_p, subln, lam_init, *, tq):
    nb, t, _ = q.shape
    qspec = pl.BlockSpec((1, tq, V7X_LANES), lambda b, h, i: (b, i, h))
    kvspec = pl.BlockSpec((1, t, V7X_LANES), lambda b, h, i: (b, 0, h))
    small = lambda shape: pl.BlockSpec(shape, lambda b, h, i: (0,) * len(shape))
    stat = pltpu.VMEM((2 * tq, V7X_LANES), F32)
    return pl.pallas_call(
        functools.partial(_diff_attn_kernel, tq=tq, lam_init=lam_init),
        grid=(nb, DIFF_HEADS, t // tq),
        in_specs=[small((4, HEAD_DIM)), small((1, 2 * HEAD_DIM)), qspec, kvspec, kvspec],
        out_specs=qspec,
        out_shape=jax.ShapeDtypeStruct((nb, t, D_MODEL), BF16),
        scratch_shapes=[stat, stat, stat],
        compiler_params=_params("arbitrary", "arbitrary", "arbitrary"),
        name="diff_attn",
    )(lam_p, subln.reshape(1, -1), q, k, v)


def _beats(row, g, tie_wins):
    return jnp.where(tie_wins, jnp.where(row >= g, 1.0, 0.0), jnp.where(row > g, 1.0, 0.0))


def _moba_select_bias(g_t, own, n_blocks):
    blk = lax.broadcasted_iota(jnp.int32, g_t.shape, 0)
    cnt = jnp.zeros(g_t.shape, F32)
    for n in range(n_blocks):
        cnt = cnt + _beats(g_t[n:n + 1, :], g_t, blk > n) * jnp.where(n < own, 1.0, 0.0)
    visible = jnp.where(cnt < MOBA_TOPK, 1.0, 0.0) * jnp.where(blk < own, 1.0, 0.0) + jnp.where(blk == own, 1.0, 0.0)
    return jnp.where(visible > 0.5, 0.0, NEG_BIG)


def _moba_attn_kernel(q_ref, k_ref, v_ref, mean_ref, o_ref, m_s, l_s, a_s, eye_s, *, tq, n_blocks):
    qi = pl.program_id(2)
    q = q_ref[0]
    lane = lax.broadcasted_iota(jnp.int32, q.shape, 1)
    zero = jnp.zeros_like(q)
    nt_dims = (((1,), (1,)), ((), ()))
    means = mean_ref[0].astype(BF16)
    own = (qi * tq + lax.broadcasted_iota(jnp.int32, (1, tq), 1)) // MOBA_BLOCK

    @pl.when(qi == 0)
    def _():
        same = lax.broadcasted_iota(jnp.int32, (tq, tq), 0) == lax.broadcasted_iota(jnp.int32, (tq, tq), 1)
        eye_s[...] = jnp.where(same, 1.0, 0.0).astype(BF16)

    eye = eye_s[...]
    pad = jnp.zeros((V7X_LANES - n_blocks, tq), BF16)
    qs, sels = [], []
    for qq in (jnp.where(lane < HEAD_DIM, q, zero), jnp.where(lane >= HEAD_DIM, q, zero)):
        g_t = lax.dot_general(means, qq, nt_dims, preferred_element_type=F32)
        bias_t = jnp.concatenate([_moba_select_bias(g_t, own, n_blocks).astype(BF16), pad], axis=0)
        sels.append(lax.dot_general(eye, bias_t, nt_dims, preferred_element_type=F32).astype(BF16))
        qs.append(qq)
    q_aug = jnp.concatenate([_stack_halves(*qs), _stack_halves(*sels)], axis=1)
    m_s[...] = jnp.full(m_s.shape, NEG_BIG, F32)
    l_s[...] = jnp.zeros(l_s.shape, F32)
    a_s[...] = jnp.zeros(a_s.shape, F32)
    half = tq // 2

    def scores(rows, koff, width):
        off = pl.multiple_of(koff, half)
        rel = (lax.broadcasted_iota(jnp.int32, (width, V7X_LANES), 1)
               - lax.broadcasted_iota(jnp.int32, (width, V7X_LANES), 0) // MOBA_BLOCK)
        onehot = jnp.where(rel == koff // MOBA_BLOCK, 1.0, 0.0).astype(BF16)
        k_aug = jnp.concatenate([k_ref[0, pl.ds(off, width), :], onehot], axis=1)
        return lax.dot_general(q_aug[rows], k_aug, nt_dims, preferred_element_type=F32)

    def values(koff, width):
        return v_ref[0, pl.ds(pl.multiple_of(koff, half), width), :]

    def step(koff, width):
        _softmax_step(scores(slice(0, 2 * tq), koff, width), values(koff, width), m_s, l_s, a_s)

    _flash_sweep(qi, tq, step, lambda koff: _diagonal_chunk(koff, tq, scores, values, m_s, l_s, a_s))
    on1, on2 = _unstack_halves(a_s[...] / jnp.sum(l_s[...], axis=1, keepdims=True))
    o_ref[0] = jnp.where(lane < HEAD_DIM, on1, on2).astype(BF16)


def _moba_attn(q, k, v, kmeans, *, tq):
    nb, t, _ = q.shape
    n_blocks = t // MOBA_BLOCK
    qspec = pl.BlockSpec((1, tq, V7X_LANES), lambda b, h, i: (b, i, h))
    kvspec = pl.BlockSpec((1, t, V7X_LANES), lambda b, h, i: (b, 0, h))
    mspec = pl.BlockSpec((1, n_blocks, V7X_LANES), lambda b, h, i: (b, 0, h))
    stat = pltpu.VMEM((2 * tq, V7X_LANES), F32)
    return pl.pallas_call(
        functools.partial(_moba_attn_kernel, tq=tq, n_blocks=n_blocks),
        grid=(nb, MOBA_HEADS // 2, t // tq),
        in_specs=[qspec, kvspec, kvspec, mspec],
        out_specs=qspec,
        out_shape=jax.ShapeDtypeStruct((nb, t, D_MODEL), BF16),
        scratch_shapes=[stat, stat, stat, pltpu.VMEM((tq, tq), BF16)],
        compiler_params=_params("arbitrary", "arbitrary", "arbitrary"),
        name="moba_attn",
    )(q, k, v, kmeans)


def _softplus(z):
    return jnp.maximum(z, 0.0) + jnp.log1p(jnp.exp(-jnp.abs(z)))


def _rg_gates(uc, wa_ref, ba_ref, wi_ref, bi_ref, lam_ref, a_ref, b_ref):
    sp = _softplus(-lam_ref[...])
    for n in range(RG_BLOCKS):
        sl = slice(n * RG_BW, (n + 1) * RG_BW)
        ub = uc[:, sl]
        ubb = ub.astype(BF16)
        r = jax.nn.sigmoid(jnp.dot(ubb, wa_ref[n], preferred_element_type=F32) + ba_ref[:, sl])
        ig = jax.nn.sigmoid(jnp.dot(ubb, wi_ref[n], preferred_element_type=F32) + bi_ref[:, sl])
        log_a = -RG_C * r * sp[:, sl]
        a = jnp.exp(log_a)
        a_ref[:, sl] = a
        b_ref[:, sl] = jnp.sqrt(1.0 - a * a) * (ig * ub)


def _rg_seq_kernel(x_ref, xh_ref, win_ref, cw_ref, cb_ref, wa_ref, ba_ref, wi_ref, bi_ref, lam_ref,
                   y_ref, hl_ref, tail_ref, gate_s, a_s, b_s, h_s, carry_s):
    i = pl.program_id(1)
    tm = x_ref.shape[0]
    w = RG_WIDTH
    xb = x_ref[...].astype(BF16)
    xh = jnp.where(i == 0, 0.0, xh_ref[...]).astype(BF16)
    gate_s[...] = jnp.dot(xb, win_ref[:, 0:w], preferred_element_type=F32)
    u = jnp.dot(xb, win_ref[:, w:2 * w], preferred_element_type=F32)
    uh = jnp.dot(xh, win_ref[:, w:2 * w], preferred_element_type=F32)
    cw = cw_ref[...]
    uc = cb_ref[...] + u * cw[3:4, :]
    for s in range(1, RG_CONV):
        uc = uc + _shift_rows(u, uh, s) * cw[3 - s:4 - s, :]
    _rg_gates(uc, wa_ref, ba_ref, wi_ref, bi_ref, lam_ref, a_s, b_s)

    @pl.when(i == 0)
    def _():
        carry_s[...] = jnp.zeros_like(carry_s)

    row = lax.broadcasted_iota(jnp.int32, (V7X_SUBLANES, w), 0)

    def group(gi, c):
        off = pl.multiple_of(gi * V7X_SUBLANES, V7X_SUBLANES)
        a8 = a_s[pl.ds(off, V7X_SUBLANES), :]
        b8 = b_s[pl.ds(off, V7X_SUBLANES), :]
        for s in (1, 2, 4):
            ok = row >= s
            b8 = jnp.where(ok, a8 * pltpu.roll(b8, s, axis=0) + b8, b8)
            a8 = jnp.where(ok, a8 * pltpu.roll(a8, s, axis=0), a8)
        h8 = a8 * carry_s[V7X_SUBLANES - 1:V7X_SUBLANES, :] + b8
        h_s[pl.ds(off, V7X_SUBLANES), :] = h8
        carry_s[...] = h8
        return c

    lax.fori_loop(0, tm // V7X_SUBLANES, group, 0)
    y_ref[...] = (jax.nn.gelu(gate_s[...], approximate=True) * h_s[...]).astype(BF16)

    @pl.when(i == pl.num_programs(1) - 1)
    def _():
        hl_ref[0] = carry_s[...]
        tail_ref[0] = u[tm - V7X_SUBLANES:tm, :]


def _rg_seq(x, win, cw, cb, wa, ba, wi, bi, lam, *, nb, tm):
    m = x.shape[0]
    nt = m // nb // tm
    hb = tm // V7X_SUBLANES
    w = RG_WIDTH
    row = pl.BlockSpec((tm, D_MODEL), lambda s, i: (s * nt + i, 0))
    halo = pl.BlockSpec((V7X_SUBLANES, D_MODEL), lambda s, i: (jnp.maximum((s * nt + i) * hb - 1, 0), 0))
    vec = _resident((1, w))
    per_seq = pl.BlockSpec((1, V7X_SUBLANES, w), lambda s, i: (s, 0, 0))
    big = pltpu.VMEM((tm, w), F32)
    return pl.pallas_call(
        _rg_seq_kernel,
        grid=(nb, nt),
        in_specs=[row, halo, _resident(win.shape), _resident(cw.shape), vec, _resident(wa.shape), vec,
                  _resident(wi.shape), vec, vec],
        out_specs=[pl.BlockSpec((tm, w), lambda s, i: (s * nt + i, 0)), per_seq, per_seq],
        out_shape=[jax.ShapeDtypeStruct((m, w), BF16), jax.ShapeDtypeStruct((nb, V7X_SUBLANES, w), F32),
                   jax.ShapeDtypeStruct((nb, V7X_SUBLANES, w), F32)],
        scratch_shapes=[big, big, big, big, pltpu.VMEM((V7X_SUBLANES, w), F32)],
        compiler_params=_params("arbitrary", "arbitrary"),
        name="rglru_seq",
    )(x, x, win, cw, cb.reshape(1, -1), wa, ba.reshape(1, -1), wi, bi.reshape(1, -1), lam.reshape(1, -1))


def _rg_state_kernel(x_ref, h0_ref, c0_ref, c1_ref, c2_ref, win_ref, cw_ref, cb_ref, wa_ref, ba_ref, wi_ref, bi_ref,
                     lam_ref, y_ref, h_ref, u_ref, a_s, b_s):
    w = RG_WIDTH
    xb = x_ref[...].astype(BF16)
    gate = jnp.dot(xb, win_ref[:, 0:w], preferred_element_type=F32)
    u = jnp.dot(xb, win_ref[:, w:2 * w], preferred_element_type=F32)
    cw = cw_ref[...]
    uc = cb_ref[...] + c0_ref[...] * cw[0:1, :] + c1_ref[...] * cw[1:2, :] + c2_ref[...] * cw[2:3, :] + u * cw[3:4, :]
    _rg_gates(uc, wa_ref, ba_ref, wi_ref, bi_ref, lam_ref, a_s, b_s)
    h = a_s[...] * h0_ref[...] + b_s[...]
    h_ref[...] = h
    u_ref[...] = u
    y_ref[...] = (jax.nn.gelu(gate, approximate=True) * h).astype(BF16)


def _rg_state(x, h0, c0, c1, c2, win, cw, cb, wa, ba, wi, bi, lam):
    m = x.shape[0]
    w = RG_WIDTH
    full = lambda shape: pl.BlockSpec(shape, lambda i: (0,) * len(shape))
    st = full((m, w))
    vec = full((1, w))
    return pl.pallas_call(
        _rg_state_kernel,
        grid=(1,),
        in_specs=[full((m, D_MODEL)), st, st, st, st, _resident(win.shape), full(cw.shape), vec,
                  _resident(wa.shape), vec, _resident(wi.shape), vec, vec],
        out_specs=[st, st, st],
        out_shape=[jax.ShapeDtypeStruct((m, w), BF16), jax.ShapeDtypeStruct((m, w), F32),
                   jax.ShapeDtypeStruct((m, w), F32)],
        scratch_shapes=[pltpu.VMEM((m, w), F32), pltpu.VMEM((m, w), F32)],
        compiler_params=_params("arbitrary"),
        name="rglru_state",
    )(x, h0, c0, c1, c2, win, cw, cb.reshape(1, -1), wa, ba.reshape(1, -1), wi, bi.reshape(1, -1), lam.reshape(1, -1))


def _page_scores(k_refs, q_b):
    return jnp.concatenate([jnp.sum(kr[0] * q_b, axis=1) for kr in k_refs], axis=1)


def _diff_decode_kernel(pt_ref, lam_ref, subln_ref, qb_ref, q_ref, kn_ref, vn_ref, *rest, n_pg, lam_init):
    del pt_ref
    k_refs, v_refs = rest[:n_pg], rest[n_pg:2 * n_pg]
    o_ref, m_s, l_s, a_s = rest[2 * n_pg:]
    step = pl.program_id(1)
    h = DIFF_HEADS

    @pl.when(step == 0)
    def _():
        s_new = jnp.sum(kn_ref[0] * q_ref[0], axis=-1, keepdims=True)
        m_s[...] = jnp.broadcast_to(s_new, m_s.shape)
        l_s[...] = jnp.full(l_s.shape, 1.0 / V7X_LANES, F32)
        a_s[...] = vn_ref[0]

    s = _page_scores(k_refs, qb_ref[0])
    m_prev = m_s[...]
    m_new = jnp.maximum(m_prev, jnp.max(s, axis=1, keepdims=True))
    alpha = jnp.exp(m_prev - m_new)
    p = jnp.exp(s - jnp.concatenate([m_new] * n_pg, axis=1))
    l_s[...] = alpha * l_s[...] + functools.reduce(
        jnp.add, [p[:, g * V7X_LANES:(g + 1) * V7X_LANES] for g in range(n_pg)])
    pb = p.astype(BF16)
    wide = PAGE_SIZE * h
    col = lax.broadcasted_iota(jnp.int32, (PAGE_SIZE, wide), 1)
    spread = jnp.where(col // h == lax.broadcasted_iota(jnp.int32, (PAGE_SIZE, wide), 0), 1.0, 0.0).astype(BF16)
    mine = (lax.broadcasted_iota(jnp.int32, (2 * h, wide), 1) % h) == (lax.broadcasted_iota(jnp.int32, (2 * h, wide), 0) // 2)
    pv = jnp.zeros(a_s.shape, F32)
    for g, vr in enumerate(v_refs):
        p_wide = jnp.dot(pb[:, g * V7X_LANES:(g + 1) * V7X_LANES], spread, preferred_element_type=F32)
        p_wide = jnp.where(mine, p_wide, 0.0).astype(BF16)
        v2 = vr[0].reshape(wide, 2 * HEAD_DIM).astype(BF16)
        pv = pv + jnp.dot(p_wide, v2, preferred_element_type=F32)
    a_s[...] = alpha * a_s[...] + pv
    m_s[...] = m_new

    @pl.when(step == pl.num_programs(1) - 1)
    def _():
        lam = _diff_lambda(lam_ref, lam_init)
        a_s[...] = a_s[...] / jnp.sum(l_s[...], axis=1, keepdims=True)
        o = a_s[pl.ds(0, h, stride=2), :] - lam * a_s[pl.ds(1, h, stride=2), :]
        o_ref[0] = _diff_finish(o, subln_ref[...], lam_init).astype(BF16)


def _decode_specs(page_table, n_pg, kshape, vshape):
    per_seq = lambda shape: pl.BlockSpec((1,) + shape, lambda b, s, pt: (b,) + (0,) * len(shape))

    def page(shape, g):
        return pl.BlockSpec((1,) + shape, lambda b, s, pt: (pt[b, s * n_pg + g],) + (0,) * len(shape))

    return per_seq, [page(kshape, g) for g in range(n_pg)] + [page(vshape, g) for g in range(n_pg)]


def _diff_decode(q, k_new, v_new, cache_k, cache_v, page_table, lam_p, subln, lam_init):
    nb, n_pages = page_table.shape
    n_pg = math.gcd(DECODE_PAGES, n_pages)
    h, c = DIFF_HEADS, 2 * DIFF_HEADS
    k_t = jnp.transpose(cache_k, (0, 2, 3, 1))
    q_b = jnp.broadcast_to(q[..., None], (nb, c, HEAD_DIM, PAGE_SIZE))
    v_rows = jnp.repeat(v_new, 2, axis=1)
    per_seq, pages = _decode_specs(page_table, n_pg, (c, HEAD_DIM, PAGE_SIZE), (PAGE_SIZE, h, 2 * HEAD_DIM))
    small = lambda shape: pl.BlockSpec(shape, lambda b, s, pt: (0,) * len(shape))
    stat = pltpu.VMEM((c, V7X_LANES), F32)
    return pl.pallas_call(
        functools.partial(_diff_decode_kernel, n_pg=n_pg, lam_init=lam_init),
        grid_spec=pltpu.PrefetchScalarGridSpec(
            num_scalar_prefetch=1,
            grid=(nb, n_pages // n_pg),
            in_specs=[small((4, HEAD_DIM)), small((1, 2 * HEAD_DIM)), per_seq((c, HEAD_DIM, PAGE_SIZE)),
                      per_seq((c, HEAD_DIM)), per_seq((c, HEAD_DIM)), per_seq((c, 2 * HEAD_DIM))] + pages,
            out_specs=per_seq((h, 2 * HEAD_DIM)),
            scratch_shapes=[stat, stat, stat],
        ),
        out_shape=jax.ShapeDtypeStruct((nb, h, 2 * HEAD_DIM), BF16),
        compiler_params=_params("arbitrary", "arbitrary"),
        name="diff_decode",
    )(page_table, lam_p, subln.reshape(1, -1), q_b, q, k_new, v_rows, *([k_t] * n_pg), *([cache_v] * n_pg))


def _moba_gate_kernel(pt_ref, qb_ref, *rest, n_pg, n_pages, n_top):
    del pt_ref
    k_refs = rest[:n_pg]
    sel_ref, s_s = rest[n_pg:]
    step = pl.program_id(1)
    c = MOBA_HEADS
    s = _page_scores(k_refs, qb_ref[0])
    for g in range(n_pg):
        s_s[step * n_pg + g] = s[:, g * V7X_LANES:(g + 1) * V7X_LANES]

    @pl.when(step == pl.num_programs(1) - 1)
    def _():
        pages_per_block = MOBA_BLOCK // PAGE_SIZE
        nblk = n_pages // pages_per_block
        gsum = jnp.sum(s_s[...], axis=2, keepdims=True).reshape(nblk, pages_per_block, c, 1)
        gate = jnp.broadcast_to(jnp.sum(gsum, axis=1), (nblk, c, V7X_LANES))
        cnt = jnp.zeros(gate.shape, F32)
        blk = lax.broadcasted_iota(jnp.int32, gate.shape, 0)
        for n in range(nblk):
            cnt = cnt + _beats(gate[n:n + 1], gate, blk > n)
        lane = lax.broadcasted_iota(jnp.int32, (c, V7X_LANES), 1)
        blk_f = blk.astype(F32)
        out = jnp.zeros((c, V7X_LANES), F32)
        for r in range(n_top):
            idx_r = jnp.sum(jnp.where(cnt == float(r), blk_f, 0.0), axis=0)
            out = jnp.where(lane == r, idx_r, out)
        sel_ref[0] = out.astype(jnp.int32)


def _moba_pick_kernel(phys_ref, qb_ref, q_ref, kn_ref, vn_ref, *rest, heads, n_chunks):
    del phys_ref
    k_refs = rest[:heads * n_chunks]
    v_refs = rest[heads * n_chunks:2 * heads * n_chunks]
    o_ref = rest[2 * heads * n_chunks]
    nt_dims = (((1,), (1,)), ((), ()))
    rows = []
    for hg in range(heads):
        q_col = qb_ref[0, 0, hg]
        ks = k_refs[hg * n_chunks:(hg + 1) * n_chunks]
        vs = v_refs[hg * n_chunks:(hg + 1) * n_chunks]
        s_new = jnp.sum(kn_ref[0, 0, hg:hg + 1, :] * q_ref[0, 0, hg:hg + 1, :], axis=-1, keepdims=True)
        s = jnp.concatenate([jnp.sum(kr[0, 0] * q_col, axis=0, keepdims=True) for kr in ks], axis=1)
        m = jnp.maximum(jnp.max(s, axis=1, keepdims=True), s_new)
        p = jnp.exp(s - m)
        w_new = jnp.exp(s_new - m)
        v_cat = jnp.concatenate([vr[0, 0] for vr in vs], axis=1).astype(BF16)
        p8 = jnp.broadcast_to(p, (V7X_SUBLANES, p.shape[1])).astype(BF16)
        pv = lax.dot_general(p8, v_cat, nt_dims, preferred_element_type=F32)[0:1, :]
        num = pv + w_new * vn_ref[0, 0, hg:hg + 1, :]
        den = jnp.sum(p, axis=1, keepdims=True) + w_new
        rows.append(num / den)
    o_ref[0, 0] = jnp.concatenate(rows, axis=0)


MOBA_PICK_HEADS = 4


def _moba_decode(q, k_new, v_new, cache_k, cache_v, page_table):
    nb, n_pages = page_table.shape
    n_pg = math.gcd(DECODE_PAGES, n_pages)
    c = MOBA_HEADS
    pages_per_block = MOBA_BLOCK // PAGE_SIZE
    assert n_pages % pages_per_block == 0
    n_top = min(MOBA_TOPK, n_pages // pages_per_block)
    k_t = jnp.transpose(cache_k, (0, 2, 3, 1))
    v_t = jnp.transpose(cache_v, (0, 2, 3, 1))
    q_b = jnp.broadcast_to(q[..., None], (nb, c, HEAD_DIM, PAGE_SIZE))
    pshape = (c, HEAD_DIM, PAGE_SIZE)
    per_seq, pages = _decode_specs(page_table, n_pg, pshape, pshape)
    sel = pl.pallas_call(
        functools.partial(_moba_gate_kernel, n_pg=n_pg, n_pages=n_pages, n_top=n_top),
        grid_spec=pltpu.PrefetchScalarGridSpec(
            num_scalar_prefetch=1,
            grid=(nb, n_pages // n_pg),
            in_specs=[per_seq(pshape)] + pages[:n_pg],
            out_specs=per_seq((c, V7X_LANES)),
            scratch_shapes=[pltpu.VMEM((n_pages, c, V7X_LANES), F32)],
        ),
        out_shape=jax.ShapeDtypeStruct((nb, c, V7X_LANES), jnp.int32),
        compiler_params=_params("arbitrary", "arbitrary"),
        name="moba_gate",
    )(page_table, q_b, *([k_t] * n_pg))
    hg = MOBA_PICK_HEADS
    n_chunks = n_top * pages_per_block
    logical = sel[:, :, :n_top, None] * pages_per_block + jnp.arange(pages_per_block, dtype=jnp.int32)
    phys = jnp.take_along_axis(page_table[:, None, :], logical.reshape(nb, c, -1), axis=2)
    grp = lambda shape: pl.BlockSpec((1, 1, hg) + shape, lambda b, g, ph: (b, g, 0) + (0,) * len(shape))

    def chunk(h, j):
        return pl.BlockSpec((1, 1, HEAD_DIM, PAGE_SIZE), lambda b, g, ph: (ph[b, g * hg + h, j], g * hg + h, 0, 0))

    chunks = [chunk(h, j) for h in range(hg) for j in range(n_chunks)]
    out = pl.pallas_call(
        functools.partial(_moba_pick_kernel, heads=hg, n_chunks=n_chunks),
        grid_spec=pltpu.PrefetchScalarGridSpec(
            num_scalar_prefetch=1,
            grid=(nb, c // hg),
            in_specs=[grp((HEAD_DIM, PAGE_SIZE)), grp((HEAD_DIM,)), grp((HEAD_DIM,)), grp((HEAD_DIM,))] + chunks + chunks,
            out_specs=pl.BlockSpec((1, 1, hg, HEAD_DIM), lambda b, g, ph: (b, g, 0, 0)),
        ),
        out_shape=jax.ShapeDtypeStruct((nb, c // hg, hg, HEAD_DIM), F32),
        compiler_params=_params("arbitrary", "arbitrary"),
        name="moba_pick",
    )(phys, q_b.reshape(nb, c // hg, hg, HEAD_DIM, PAGE_SIZE), q.reshape(nb, c // hg, hg, HEAD_DIM),
      k_new.reshape(nb, c // hg, hg, HEAD_DIM), v_new.reshape(nb, c // hg, hg, HEAD_DIM),
      *([k_t] * (hg * n_chunks)), *([v_t] * (hg * n_chunks)))
    return out.reshape(nb, c, HEAD_DIM).astype(BF16)


def kernel(x_prompt, x_sample, cache_k_l0, cache_v_l0, cache_k_l1, cache_v_l1, cache_k_l3, cache_v_l3, state_rg_h_l2, state_rg_conv_l2, state_ffn_conv, page_table, l0_w_qkv, l0_lambda, l0_subln, l0_w_o, l1_w_qkv, l1_w_o, l2_w_in, l2_conv_w, l2_conv_b, l2_w_a, l2_b_a, l2_w_i, l2_b_i, l2_lambda, l2_w_o, l3_w_qkv, l3_lambda, l3_subln, l3_w_o, ffn_w_up, ffn_conv_w, ffn_conv_b, ffn_w_down, ln_mix_g, ln_mix_b, ln_ffn_g, ln_ffn_b):
    nb, t, d = x_prompt.shape
    ns = x_sample.shape[0]
    past_len = page_table.shape[1] * PAGE_SIZE
    tm = min(ROW_TILE, t)
    tab_p = _rope_tables(jnp.arange(t, dtype=jnp.int32))
    tab_s = _rope_tables(jnp.full((ns,), past_len, dtype=jnp.int32))
    xp = x_prompt.reshape(nb * t, d)
    xs = x_sample.reshape(ns, d)
    outs_p, outs_s, ffn_p, ffn_s = {}, {}, [], []

    def attn_layer(i, w_qkv, w_o, cache_k, cache_v, lam_p=None, subln=None):
        nonlocal xp, xs
        wq = w_qkv.astype(BF16)
        wo = w_o.astype(BF16)
        diff = lam_p is not None
        res_p = _qkv_rope(xp, wq, tab_p, tm=tm, rows_per_seq=t, with_means=not diff, k_t=True, v_t=not diff)
        q_p, kb_p, vb_p, k_p, v_p = res_p[:5]
        q_s, _, _, k_s, v_s = _qkv_rope(xs, wq, tab_s, tm=ns, rows_per_seq=ns, with_means=False)
        shp = lambda a: a.reshape(nb, t, d)
        q_s3 = q_s.astype(F32).reshape(ns, 2 * DIFF_HEADS, HEAD_DIM)
        k_s3 = k_s.reshape(ns, 2 * DIFF_HEADS, HEAD_DIM)
        if diff:
            lam_init = 0.8 - 0.6 * math.exp(-0.3 * i)
            o_p = _diff_attn(shp(q_p), shp(kb_p), shp(vb_p), lam_p, subln, lam_init, tq=min(ATT_TQ, t))
            v_s3 = v_s.reshape(ns, DIFF_HEADS, 2 * HEAD_DIM)
            o_s = _diff_decode(q_s3, k_s3, v_s3, cache_k, cache_v, page_table, lam_p, subln, lam_init)
            kv_shape = ((2 * DIFF_HEADS, HEAD_DIM), (DIFF_HEADS, 2 * HEAD_DIM))
        else:
            kmeans = res_p[5].reshape(nb, t // MOBA_BLOCK, d)
            o_p = _moba_attn(shp(q_p), shp(kb_p), shp(vb_p), kmeans, tq=min(ATT_TQ, t))
            v_s3 = v_s.reshape(ns, MOBA_HEADS, HEAD_DIM)
            o_s = _moba_decode(q_s3, k_s3, v_s3, cache_k, cache_v, page_table)
            kv_shape = ((MOBA_HEADS, HEAD_DIM), (MOBA_HEADS, HEAD_DIM))
        back = lambda a: jnp.transpose(a, (0, 3, 1, 2))
        outs_p[i] = (back(k_p), v_p.reshape(nb, t, *kv_shape[1]) if diff else back(v_p))
        outs_s[i] = (k_s.reshape(ns, 1, *kv_shape[0]), v_s.reshape(ns, 1, *kv_shape[1]))
        xp = _proj_ln(o_p.reshape(nb * t, d), xp, wo, ln_mix_g[i], ln_mix_b[i], tm=tm)
        xs = _proj_ln(o_s.reshape(ns, d), xs, wo, ln_mix_g[i], ln_mix_b[i], tm=ns)

    def rg_layer(i):
        nonlocal xp, xs
        win = l2_w_in.astype(BF16)
        wa = l2_w_a.astype(BF16)
        wi = l2_w_i.astype(BF16)
        wo = l2_w_o.astype(BF16)
        y_p, hl_p, tail_p = _rg_seq(xp, win, l2_conv_w, l2_conv_b, wa, l2_b_a, wi, l2_b_i, l2_lambda, nb=nb, tm=tm)
        cs = state_rg_conv_l2
        y_s, h_s, u_s = _rg_state(xs, state_rg_h_l2, cs[:, 0], cs[:, 1], cs[:, 2], win, l2_conv_w, l2_conv_b,
                                  wa, l2_b_a, wi, l2_b_i, l2_lambda)
        outs_p[i] = (hl_p[:, V7X_SUBLANES - 1], tail_p[:, V7X_SUBLANES - (RG_CONV - 1):])
        outs_s[i] = (h_s, jnp.concatenate([cs[:, 1:], u_s[:, None]], axis=1))
        xp = _proj_ln(y_p, xp, wo, ln_mix_g[i], ln_mix_b[i], tm=tm)
        xs = _proj_ln(y_s, xs, wo, ln_mix_g[i], ln_mix_b[i], tm=ns)

    def ffn_layer(i):
        nonlocal xp, xs
        wup = ffn_w_up[i].astype(BF16)
        wdn = ffn_w_down[i].astype(BF16)
        xp, tail = _ffn_seq(xp, wup, wdn, ffn_conv_w[i], ffn_conv_b[i], ln_ffn_g[i], ln_ffn_b[i], nb=nb,
                            tm=min(FFN_ROW_TILE, t))
        st = state_ffn_conv[i]
        xs, a_s = _ffn_state(xs, st[:, 0], st[:, 1], wup, wdn, ffn_conv_w[i], ffn_conv_b[i], ln_ffn_g[i], ln_ffn_b[i])
        ffn_p.append(tail[:, V7X_SUBLANES - (FFN_CONV - 1):])
        ffn_s.append(jnp.concatenate([st[:, 1:], a_s[:, None]], axis=1))

    attn_layer(0, l0_w_qkv, l0_w_o, cache_k_l0, cache_v_l0, l0_lambda, l0_subln)
    ffn_layer(0)
    attn_layer(1, l1_w_qkv, l1_w_o, cache_k_l1, cache_v_l1)
    ffn_layer(1)
    rg_layer(2)
    ffn_layer(2)
    attn_layer(3, l3_w_qkv, l3_w_o, cache_k_l3, cache_v_l3, l3_lambda, l3_subln)
    ffn_layer(3)

    return (xp.reshape(nb, t, d), xs.reshape(ns, 1, d),
            outs_p[0][0], outs_p[0][1], outs_p[1][0], outs_p[1][1], outs_p[2][0], outs_p[2][1],
            outs_p[3][0], outs_p[3][1], jnp.stack(ffn_p),
            outs_s[0][0], outs_s[0][1], outs_s[1][0], outs_s[1][1], outs_s[2][0], outs_s[2][1],
            outs_s[3][0], outs_s[3][1], jnp.stack(ffn_s))
```

```python
import functools
import math

import jax
import jax.numpy as jnp
from jax import lax
from jax.experimental import pallas as pl
from jax.experimental.pallas import tpu as pltpu

F32 = jnp.float32
BF16 = jnp.bfloat16

D_MODEL = 1024
DEPTH = 4
PAGE_SIZE = 128
HEAD_DIM = 64
DIFF_HEADS = D_MODEL // (2 * HEAD_DIM)
MOBA_HEADS = D_MODEL // HEAD_DIM
MOBA_BLOCK = 256
MOBA_TOPK = 3
RG_WIDTH = 3 * D_MODEL // 2
RG_BLOCKS = 12
RG_BW = RG_WIDTH // RG_BLOCKS
RG_CONV = 4
RG_C = 8.0
FFN_DIM = 2816
FFN_CONV = 3
ROPE_THETA = 10000.0
NORM_EPS = 1e-5
DEEPNORM_ALPHA = (2.0 * DEPTH) ** 0.25

V7X_LANES = 128
V7X_SUBLANES = 8
V7X_VMEM_BYTES = 64 * 1024 * 1024
VMEM_LIMIT = V7X_VMEM_BYTES * 7 // 8

NEG_BIG = -1e30

ROW_TILE = 512
FFN_ROW_TILE = 1024
FFN_CHUNK = 512
ATT_TQ = 512
DECODE_PAGES = 16


def _params(*sem):
    return pltpu.CompilerParams(dimension_semantics=sem, vmem_limit_bytes=VMEM_LIMIT)


def _resident(shape):
    nd = len(shape)
    return pl.BlockSpec(shape, lambda *_: (0,) * nd, pipeline_mode=pl.Buffered(1))


def _layer_norm(z, g, b):
    mu = jnp.mean(z, axis=-1, keepdims=True)
    zc = z - mu
    var = jnp.mean(zc * zc, axis=-1, keepdims=True)
    return zc * lax.rsqrt(var + NORM_EPS) * g + b


def _shift_rows(a, halo, s):
    row = lax.broadcasted_iota(jnp.int32, a.shape, 0)
    out = pltpu.roll(a, s, axis=0)
    hn = halo.shape[0]
    for r in range(s):
        out = jnp.where(row == r, halo[hn - s + r:hn - s + r + 1, :], out)
    return out


def _rope_cols(t, cos, sa, sb):
    return t * cos + pltpu.roll(t, HEAD_DIM // 2, axis=1) * sa + pltpu.roll(t, V7X_LANES - HEAD_DIM // 2, axis=1) * sb


def _store_heads(out_ref, cols, x, transposed):
    if transposed:
        pair = V7X_LANES // HEAD_DIM
        first = cols.start // HEAD_DIM
        out_ref[0, first:first + pair, :, :] = x.T.reshape(pair, HEAD_DIM, x.shape[0])
    else:
        out_ref[:, cols] = x


def _qkv_rope_kernel(x_ref, w_ref, cos_ref, sa_ref, sb_ref, q_ref, kb_ref, vb_ref, k_ref, v_ref, *rest,
                     q_scale, n_mean_blocks, k_t, v_t):
    xb = x_ref[...].astype(BF16)
    cos, sa, sb = cos_ref[...], sa_ref[...], sb_ref[...]
    d = D_MODEL
    q = jnp.dot(xb, w_ref[:, 0:d], preferred_element_type=F32)
    k = jnp.dot(xb, w_ref[:, d:2 * d], preferred_element_type=F32)
    v = jnp.dot(xb, w_ref[:, 2 * d:3 * d], preferred_element_type=F32)
    vb_ref[...] = v.astype(BF16)
    sums = [[] for _ in range(n_mean_blocks)]
    for c in range(d // V7X_LANES):
        sl = slice(c * V7X_LANES, (c + 1) * V7X_LANES)
        qc = _rope_cols(q[:, sl], cos, sa, sb)
        kc = _rope_cols(k[:, sl], cos, sa, sb)
        q_ref[:, sl] = (qc * q_scale).astype(BF16)
        kb_ref[:, sl] = kc.astype(BF16)
        _store_heads(k_ref, sl, kc, k_t)
        _store_heads(v_ref, sl, v[:, sl], v_t)
        for r in range(n_mean_blocks):
            sums[r].append(jnp.mean(kc[r * MOBA_BLOCK:(r + 1) * MOBA_BLOCK, :], axis=0, keepdims=True))
    for r in range(n_mean_blocks):
        rest[0][r] = jnp.concatenate(sums[r], axis=1)


def _rope_tables(pos):
    half = HEAD_DIM // 2
    inv_freq = ROPE_THETA ** (-jnp.arange(half, dtype=F32) / half)
    ang = pos.astype(F32)[:, None] * inv_freq[None, :]
    cos = jnp.cos(ang)
    sin = jnp.sin(ang)
    zero = jnp.zeros_like(sin)
    cos128 = jnp.tile(cos, (1, 4))
    sa128 = jnp.tile(jnp.concatenate([zero, sin], axis=1), (1, 2))
    sb128 = jnp.tile(jnp.concatenate([-sin, zero], axis=1), (1, 2))
    return cos128, sa128, sb128


def _qkv_rope(x, w, tables, *, tm, rows_per_seq, with_means, k_t=False, v_t=False):
    m = x.shape[0]
    nt = rows_per_seq // tm
    n_mean = tm // MOBA_BLOCK if with_means else 0
    row = pl.BlockSpec((tm, D_MODEL), lambda i: (i, 0))
    tab = pl.BlockSpec((tm, V7X_LANES), lambda i: (i % nt, 0))
    heads = D_MODEL // HEAD_DIM
    row_f32 = jax.ShapeDtypeStruct((m, D_MODEL), F32)
    t_f32 = jax.ShapeDtypeStruct((m // rows_per_seq, heads, HEAD_DIM, rows_per_seq), F32)
    t_spec = pl.BlockSpec((1, heads, HEAD_DIM, tm), lambda i: (i // nt, 0, 0, i % nt))
    bf = jax.ShapeDtypeStruct((m, D_MODEL), BF16)
    out_shape = [bf, bf, bf, t_f32 if k_t else row_f32, t_f32 if v_t else row_f32]
    out_specs = [row, row, row, t_spec if k_t else row, t_spec if v_t else row]
    if with_means:
        out_shape.append(jax.ShapeDtypeStruct((m // MOBA_BLOCK, 1, D_MODEL), F32))
        out_specs.append(pl.BlockSpec((n_mean, 1, D_MODEL), lambda i: (i, 0, 0)))
    return pl.pallas_call(
        functools.partial(_qkv_rope_kernel, q_scale=HEAD_DIM ** -0.5, n_mean_blocks=n_mean, k_t=k_t, v_t=v_t),
        grid=(m // tm,),
        in_specs=[row, _resident(w.shape), tab, tab, tab],
        out_specs=out_specs,
        out_shape=out_shape,
        compiler_params=_params("arbitrary"),
        name="qkv_rope",
    )(x, w, *tables)


def _proj_ln_kernel(o_ref, x_ref, w_ref, g_ref, b_ref, out_ref):
    y = jnp.dot(o_ref[...], w_ref[...], preferred_element_type=F32)
    out_ref[...] = _layer_norm(DEEPNORM_ALPHA * x_ref[...] + y, g_ref[...], b_ref[...])


def _proj_ln(o, x, w, g, b, *, tm):
    m, kd = o.shape
    row_o = pl.BlockSpec((tm, kd), lambda i: (i, 0))
    row_x = pl.BlockSpec((tm, D_MODEL), lambda i: (i, 0))
    vec = pl.BlockSpec((1, D_MODEL), lambda i: (0, 0))
    return pl.pallas_call(
        _proj_ln_kernel,
        grid=(m // tm,),
        in_specs=[row_o, row_x, _resident(w.shape), vec, vec],
        out_specs=row_x,
        out_shape=jax.ShapeDtypeStruct((m, D_MODEL), F32),
        compiler_params=_params("arbitrary"),
        name="proj_ln",
    )(o, x, w, g.reshape(1, -1), b.reshape(1, -1))


def _ffn_chunks(xb, xh, p_refs, wup_ref, wdn_ref, cw_ref, cb_ref, acc_ref, a_out_ref, write_tail):
    tm = xb.shape[0]

    def chunk(off, width):
        wa = wup_ref[:, off:off + width]
        wg = wup_ref[:, FFN_DIM + off:FFN_DIM + off + width]
        a = jnp.dot(xb, wa, preferred_element_type=F32)
        g = jnp.dot(xb, wg, preferred_element_type=F32)
        if xh is not None:
            ah = jnp.dot(xh, wa, preferred_element_type=F32)
            p1 = _shift_rows(a, ah, 1)
            p2 = _shift_rows(a, ah, 2)
        else:
            p2 = p_refs[0][:, off:off + width]
            p1 = p_refs[1][:, off:off + width]
        cw = cw_ref[:, off:off + width]
        cb = cb_ref[:, off:off + width]
        ac = cb + p2 * cw[0:1, :] + p1 * cw[1:2, :] + a * cw[2:3, :]
        acc_ref[:, off:off + width] = (jax.nn.gelu(ac, approximate=True) * g).astype(BF16)
        if write_tail is None:
            a_out_ref[:, off:off + width] = a
        else:
            @pl.when(write_tail)
            def _():
                a_out_ref[0, :, off:off + width] = a[tm - V7X_SUBLANES:tm, :]

    for off in range(0, FFN_DIM, FFN_CHUNK):
        chunk(off, min(FFN_CHUNK, FFN_DIM - off))
    return jnp.dot(acc_ref[...], wdn_ref[...], preferred_element_type=F32)


def _ffn_seq_kernel(x_ref, xh_ref, wup_ref, wdn_ref, cw_ref, cb_ref, g_ref, b_ref, out_ref, tail_ref, acc_ref):
    i = pl.program_id(1)
    x = x_ref[...]
    xb = x.astype(BF16)
    xh = jnp.where(i == 0, 0.0, xh_ref[...]).astype(BF16)
    y = _ffn_chunks(xb, xh, None, wup_ref, wdn_ref, cw_ref, cb_ref, acc_ref, tail_ref, i == pl.num_programs(1) - 1)
    out_ref[...] = _layer_norm(DEEPNORM_ALPHA * x + y, g_ref[...], b_ref[...])


def _ffn_state_kernel(x_ref, p2_ref, p1_ref, wup_ref, wdn_ref, cw_ref, cb_ref, g_ref, b_ref, out_ref, a_ref, acc_ref):
    x = x_ref[...]
    y = _ffn_chunks(x.astype(BF16), None, (p2_ref, p1_ref), wup_ref, wdn_ref, cw_ref, cb_ref, acc_ref, a_ref, None)
    out_ref[...] = _layer_norm(DEEPNORM_ALPHA * x + y, g_ref[...], b_ref[...])


def _ffn_seq(x, wup, wdn, cw, cb, g, b, *, nb, tm):
    m = x.shape[0]
    nt = m // nb // tm
    hb = tm // V7X_SUBLANES
    row = pl.BlockSpec((tm, D_MODEL), lambda s, i: (s * nt + i, 0))
    halo = pl.BlockSpec((V7X_SUBLANES, D_MODEL), lambda s, i: (jnp.maximum((s * nt + i) * hb - 1, 0), 0))
    vec_d = pl.BlockSpec((1, D_MODEL), lambda s, i: (0, 0))
    return pl.pallas_call(
        _ffn_seq_kernel,
        grid=(nb, nt),
        in_specs=[row, halo, _resident(wup.shape), _resident(wdn.shape), _resident(cw.shape),
                  _resident((1, FFN_DIM)), vec_d, vec_d],
        out_specs=[row, pl.BlockSpec((1, V7X_SUBLANES, FFN_DIM), lambda s, i: (s, 0, 0))],
        out_shape=[jax.ShapeDtypeStruct((m, D_MODEL), F32),
                   jax.ShapeDtypeStruct((nb, V7X_SUBLANES, FFN_DIM), F32)],
        scratch_shapes=[pltpu.VMEM((tm, FFN_DIM), BF16)],
        compiler_params=_params("arbitrary", "arbitrary"),
        name="conv_ffn_seq",
    )(x, x, wup, wdn, cw, cb.reshape(1, -1), g.reshape(1, -1), b.reshape(1, -1))


def _ffn_state(x, p2, p1, wup, wdn, cw, cb, g, b):
    m = x.shape[0]
    full = lambda shape: pl.BlockSpec(shape, lambda i: (0,) * len(shape))
    return pl.pallas_call(
        _ffn_state_kernel,
        grid=(1,),
        in_specs=[full((m, D_MODEL)), full((m, FFN_DIM)), full((m, FFN_DIM)), _resident(wup.shape),
                  _resident(wdn.shape), _resident(cw.shape), _resident((1, FFN_DIM)),
                  full((1, D_MODEL)), full((1, D_MODEL))],
        out_specs=[full((m, D_MODEL)), full((m, FFN_DIM))],
        out_shape=[jax.ShapeDtypeStruct((m, D_MODEL), F32), jax.ShapeDtypeStruct((m, FFN_DIM), F32)],
        scratch_shapes=[pltpu.VMEM((m, FFN_DIM), BF16)],
        compiler_params=_params("arbitrary"),
        name="conv_ffn_state",
    )(x, p2, p1, wup, wdn, cw, cb.reshape(1, -1), g.reshape(1, -1), b.reshape(1, -1))


def _softmax_step(s, v, m_ref, l_ref, acc_ref):
    nl = s.shape[1] // V7X_LANES
    tiles = [s[:, j * V7X_LANES:(j + 1) * V7X_LANES] for j in range(nl)]
    m_prev = m_ref[...]
    m_new = jnp.maximum(m_prev, jnp.max(functools.reduce(jnp.maximum, tiles), axis=1, keepdims=True))
    alpha = jnp.exp(m_prev - m_new)
    ps = [jnp.exp(t - m_new) for t in tiles]
    l_ref[...] = alpha * l_ref[...] + functools.reduce(jnp.add, ps)
    p = jnp.concatenate(ps, axis=1).astype(BF16)
    acc_ref[...] = alpha * acc_ref[...] + jnp.dot(p, v, preferred_element_type=F32)
    m_ref[...] = m_new


def _diff_lambda(lam_ref, lam_init):
    lp = lam_ref[...]
    s1 = jnp.sum(lp[0:1, :] * lp[1:2, :], axis=1, keepdims=True)
    s2 = jnp.sum(lp[2:3, :] * lp[3:4, :], axis=1, keepdims=True)
    return jnp.exp(s1) - jnp.exp(s2) + lam_init


def _diff_finish(o, subln, lam_init):
    ms = jnp.mean(o * o, axis=-1, keepdims=True)
    return o * lax.rsqrt(ms + NORM_EPS) * subln * (1.0 - lam_init)


def _flash_sweep(qi, tq, step, diagonal):
    lax.fori_loop(0, qi // 4, lambda j, c: (step(j * (4 * tq), 4 * tq), c)[1], 0)

    @pl.when(qi % 4 >= 2)
    def _():
        step((qi // 4) * (4 * tq), 2 * tq)

    @pl.when(qi % 2 == 1)
    def _():
        step((qi - 1) * tq, tq)

    diagonal(qi * tq)


def _stack_halves(x1, x2):
    half = x1.shape[0] // 2
    return jnp.concatenate([x1[0:half], x2[0:half], x1[half:], x2[half:]], axis=0)


def _unstack_halves(x):
    q = x.shape[0] // 4
    return (jnp.concatenate([x[0:q], x[2 * q:3 * q]], axis=0), jnp.concatenate([x[q:2 * q], x[3 * q:]], axis=0))


def _diagonal_chunk(koff, tq, scores, values, m_s, l_s, a_s):
    half = tq // 2
    tri = lax.broadcasted_iota(jnp.int32, (half, half), 1) <= lax.broadcasted_iota(jnp.int32, (half, half), 0)
    tri = jnp.where(tri, 1.0, 0.0)
    ones = jnp.ones((half, half), F32)
    keep_a = jnp.concatenate([tri, tri, ones, ones], axis=0) > 0.5
    s_a = jnp.where(keep_a, scores(slice(0, 2 * tq), koff, half), NEG_BIG)
    _softmax_step(s_a, values(koff, half), m_s, l_s, a_s)
    lo = slice(tq, 2 * tq)
    keep_b = jnp.concatenate([tri, tri], axis=0) > 0.5
    s_b = jnp.where(keep_b, scores(lo, koff + half, half), NEG_BIG)
    _softmax_step(s_b, values(koff + half, half), m_s.at[lo], l_s.at[lo], a_s.at[lo])


def _diff_attn_kernel(lam_ref, subln_ref, q_ref, k_ref, v_ref, o_ref, m_s, l_s, a_s, *, tq, lam_init):
    qi = pl.program_id(2)
    q = q_ref[0]
    lane = lax.broadcasted_iota(jnp.int32, q.shape, 1)
    zero = jnp.zeros_like(q)
    q12 = _stack_halves(jnp.where(lane < HEAD_DIM, q, zero), jnp.where(lane >= HEAD_DIM, q, zero))
    m_s[...] = jnp.full(m_s.shape, NEG_BIG, F32)
    l_s[...] = jnp.zeros(l_s.shape, F32)
    a_s[...] = jnp.zeros(a_s.shape, F32)
    nt_dims = (((1,), (1,)), ((), ()))
    half = tq // 2

    def scores(rows, koff, width):
        off = pl.multiple_of(koff, half)
        return lax.dot_general(q12[rows], k_ref[0, pl.ds(off, width), :], nt_dims, preferred_element_type=F32)

    def values(koff, width):
        return v_ref[0, pl.ds(pl.multiple_of(koff, half), width), :]

    def step(koff, width):
        _softmax_step(scores(slice(0, 2 * tq), koff, width), values(koff, width), m_s, l_s, a_s)

    _flash_sweep(qi, tq, step, lambda koff: _diagonal_chunk(koff, tq, scores, values, m_s, l_s, a_s))

    lam = _diff_lambda(lam_ref, lam_init)
    on1, on2 = _unstack_halves(a_s[...] / jnp.sum(l_s[...], axis=1, keepdims=True))
    o_ref[0] = _diff_finish(on1 - lam * on2, subln_ref[...], lam_init).astype(BF16)


def _diff_attn(q, k, v, lam_p, subln, lam_init, *, tq):
    nb, t, _ = q.shape
    qspec = pl.BlockSpec((1, tq, V7X_LANES), lambda b, h, i: (b, i, h))
    kvspec = pl.BlockSpec((1, t, V7X_LANES), lambda b, h, i: (b, 0, h))
    small = lambda shape: pl.BlockSpec(shape, lambda b, h, i: (0,) * len(shape))
    stat = pltpu.VMEM((2 * tq, V7X_LANES), F32)
    return pl.pallas_call(
        functools.partial(_diff_attn_kernel, tq=tq, lam_init=lam_init),
        grid=(nb, DIFF_HEADS, t // tq),
        in_specs=[small((4, HEAD_DIM)), small((1, 2 * HEAD_DIM)), qspec, kvspec, kvspec],
        out_specs=qspec,
        out_shape=jax.ShapeDtypeStruct((nb, t, D_MODEL), BF16),
        scratch_shapes=[stat, stat, stat],
        compiler_params=_params("arbitrary", "arbitrary", "arbitrary"),
        name="diff_attn",
    )(lam_p, subln.reshape(1, -1), q, k, v)


def _beats(row, g, tie_wins):
    return jnp.where(tie_wins, jnp.where(row >= g, 1.0, 0.0), jnp.where(row > g, 1.0, 0.0))


def _moba_select_bias(g_t, own, n_blocks):
    blk = lax.broadcasted_iota(jnp.int32, g_t.shape, 0)
    cnt = jnp.zeros(g_t.shape, F32)
    for n in range(n_blocks):
        cnt = cnt + _beats(g_t[n:n + 1, :], g_t, blk > n) * jnp.where(n < own, 1.0, 0.0)
    visible = jnp.where(cnt < MOBA_TOPK, 1.0, 0.0) * jnp.where(blk < own, 1.0, 0.0) + jnp.where(blk == own, 1.0, 0.0)
    return jnp.where(visible > 0.5, 0.0, NEG_BIG)


def _moba_attn_kernel(q_ref, k_ref, v_ref, mean_ref, o_ref, m_s, l_s, a_s, eye_s, *, tq, n_blocks):
    qi = pl.program_id(2)
    q = q_ref[0]
    lane = lax.broadcasted_iota(jnp.int32, q.shape, 1)
    zero = jnp.zeros_like(q)
    nt_dims = (((1,), (1,)), ((), ()))
    means = mean_ref[0].astype(BF16)
    own = (qi * tq + lax.broadcasted_iota(jnp.int32, (1, tq), 1)) // MOBA_BLOCK

    @pl.when(qi == 0)
    def _():
        same = lax.broadcasted_iota(jnp.int32, (tq, tq), 0) == lax.broadcasted_iota(jnp.int32, (tq, tq), 1)
        eye_s[...] = jnp.where(same, 1.0, 0.0).astype(BF16)

    eye = eye_s[...]
    pad = jnp.zeros((V7X_LANES - n_blocks, tq), BF16)
    qs, sels = [], []
    for qq in (jnp.where(lane < HEAD_DIM, q, zero), jnp.where(lane >= HEAD_DIM, q, zero)):
        g_t = lax.dot_general(means, qq, nt_dims, preferred_element_type=F32)
        bias_t = jnp.concatenate([_moba_select_bias(g_t, own, n_blocks).astype(BF16), pad], axis=0)
        sels.append(lax.dot_general(eye, bias_t, nt_dims, preferred_element_type=F32).astype(BF16))
        qs.append(qq)
    q_aug = jnp.concatenate([_stack_halves(*qs), _stack_halves(*sels)], axis=1)
    m_s[...] = jnp.full(m_s.shape, NEG_BIG, F32)
    l_s[...] = jnp.zeros(l_s.shape, F32)
    a_s[...] = jnp.zeros(a_s.shape, F32)
    half = tq // 2

    def scores(rows, koff, width):
        off = pl.multiple_of(koff, half)
        rel = (lax.broadcasted_iota(jnp.int32, (width, V7X_LANES), 1)
               - lax.broadcasted_iota(jnp.int32, (width, V7X_LANES), 0) // MOBA_BLOCK)
        onehot = jnp.where(rel == koff // MOBA_BLOCK, 1.0, 0.0).astype(BF16)
        k_aug = jnp.concatenate([k_ref[0, pl.ds(off, width), :], onehot], axis=1)
        return lax.dot_general(q_aug[rows], k_aug, nt_dims, preferred_element_type=F32)

    def values(koff, width):
        return v_ref[0, pl.ds(pl.multiple_of(koff, half), width), :]

    def step(koff, width):
        _softmax_step(scores(slice(0, 2 * tq), koff, width), values(koff, width), m_s, l_s, a_s)

    _flash_sweep(qi, tq, step, lambda koff: _diagonal_chunk(koff, tq, scores, values, m_s, l_s, a_s))
    on1, on2 = _unstack_halves(a_s[...] / jnp.sum(l_s[...], axis=1, keepdims=True))
    o_ref[0] = jnp.where(lane < HEAD_DIM, on1, on2).astype(BF16)


def _moba_attn(q, k, v, kmeans, *, tq):
    nb, t, _ = q.shape
    n_blocks = t // MOBA_BLOCK
    qspec = pl.BlockSpec((1, tq, V7X_LANES), lambda b, h, i: (b, i, h))
    kvspec = pl.BlockSpec((1, t, V7X_LANES), lambda b, h, i: (b, 0, h))
    mspec = pl.BlockSpec((1, n_blocks, V7X_LANES), lambda b, h, i: (b, 0, h))
    stat = pltpu.VMEM((2 * tq, V7X_LANES), F32)
    return pl.pallas_call(
        functools.partial(_moba_attn_kernel, tq=tq, n_blocks=n_blocks),
        grid=(nb, MOBA_HEADS // 2, t // tq),
        in_specs=[qspec, kvspec, kvspec, mspec],
        out_specs=qspec,
        out_shape=jax.ShapeDtypeStruct((nb, t, D_MODEL), BF16),
        scratch_shapes=[stat, stat, stat, pltpu.VMEM((tq, tq), BF16)],
        compiler_params=_params("arbitrary", "arbitrary", "arbitrary"),
        name="moba_attn",
    )(q, k, v, kmeans)


def _softplus(z):
    return jnp.maximum(z, 0.0) + jnp.log1p(jnp.exp(-jnp.abs(z)))


def _rg_gates(uc, wa_ref, ba_ref, wi_ref, bi_ref, lam_ref, a_ref, b_ref):
    sp = _softplus(-lam_ref[...])
    for n in range(RG_BLOCKS):
        sl = slice(n * RG_BW, (n + 1) * RG_BW)
        ub = uc[:, sl]
        ubb = ub.astype(BF16)
        r = jax.nn.sigmoid(jnp.dot(ubb, wa_ref[n], preferred_element_type=F32) + ba_ref[:, sl])
        ig = jax.nn.sigmoid(jnp.dot(ubb, wi_ref[n], preferred_element_type=F32) + bi_ref[:, sl])
        log_a = -RG_C * r * sp[:, sl]
        a = jnp.exp(log_a)
        a_ref[:, sl] = a
        b_ref[:, sl] = jnp.sqrt(1.0 - a * a) * (ig * ub)


def _rg_seq_kernel(x_ref, xh_ref, win_ref, cw_ref, cb_ref, wa_ref, ba_ref, wi_ref, bi_ref, lam_ref,
                   y_ref, hl_ref, tail_ref, gate_s, a_s, b_s, h_s, carry_s):
    i = pl.program_id(1)
    tm = x_ref.shape[0]
    w = RG_WIDTH
    xb = x_ref[...].astype(BF16)
    xh = jnp.where(i == 0, 0.0, xh_ref[...]).astype(BF16)
    gate_s[...] = jnp.dot(xb, win_ref[:, 0:w], preferred_element_type=F32)
    u = jnp.dot(xb, win_ref[:, w:2 * w], preferred_element_type=F32)
    uh = jnp.dot(xh, win_ref[:, w:2 * w], preferred_element_type=F32)
    cw = cw_ref[...]
    uc = cb_ref[...] + u * cw[3:4, :]
    for s in range(1, RG_CONV):
        uc = uc + _shift_rows(u, uh, s) * cw[3 - s:4 - s, :]
    _rg_gates(uc, wa_ref, ba_ref, wi_ref, bi_ref, lam_ref, a_s, b_s)

    @pl.when(i == 0)
    def _():
        carry_s[...] = jnp.zeros_like(carry_s)

    row = lax.broadcasted_iota(jnp.int32, (V7X_SUBLANES, w), 0)

    def group(gi, c):
        off = pl.multiple_of(gi * V7X_SUBLANES, V7X_SUBLANES)
        a8 = a_s[pl.ds(off, V7X_SUBLANES), :]
        b8 = b_s[pl.ds(off, V7X_SUBLANES), :]
        for s in (1, 2, 4):
            ok = row >= s
            b8 = jnp.where(ok, a8 * pltpu.roll(b8, s, axis=0) + b8, b8)
            a8 = jnp.where(ok, a8 * pltpu.roll(a8, s, axis=0), a8)
        h8 = a8 * carry_s[V7X_SUBLANES - 1:V7X_SUBLANES, :] + b8
        h_s[pl.ds(off, V7X_SUBLANES), :] = h8
        carry_s[...] = h8
        return c

    lax.fori_loop(0, tm // V7X_SUBLANES, group, 0)
    y_ref[...] = (jax.nn.gelu(gate_s[...], approximate=True) * h_s[...]).astype(BF16)

    @pl.when(i == pl.num_programs(1) - 1)
    def _():
        hl_ref[0] = carry_s[...]
        tail_ref[0] = u[tm - V7X_SUBLANES:tm, :]


def _rg_seq(x, win, cw, cb, wa, ba, wi, bi, lam, *, nb, tm):
    m = x.shape[0]
    nt = m // nb // tm
    hb = tm // V7X_SUBLANES
    w = RG_WIDTH
    row = pl.BlockSpec((tm, D_MODEL), lambda s, i: (s * nt + i, 0))
    halo = pl.BlockSpec((V7X_SUBLANES, D_MODEL), lambda s, i: (jnp.maximum((s * nt + i) * hb - 1, 0), 0))
    vec = _resident((1, w))
    per_seq = pl.BlockSpec((1, V7X_SUBLANES, w), lambda s, i: (s, 0, 0))
    big = pltpu.VMEM((tm, w), F32)
    return pl.pallas_call(
        _rg_seq_kernel,
        grid=(nb, nt),
        in_specs=[row, halo, _resident(win.shape), _resident(cw.shape), vec, _resident(wa.shape), vec,
                  _resident(wi.shape), vec, vec],
        out_specs=[pl.BlockSpec((tm, w), lambda s, i: (s * nt + i, 0)), per_seq, per_seq],
        out_shape=[jax.ShapeDtypeStruct((m, w), BF16), jax.ShapeDtypeStruct((nb, V7X_SUBLANES, w), F32),
                   jax.ShapeDtypeStruct((nb, V7X_SUBLANES, w), F32)],
        scratch_shapes=[big, big, big, big, pltpu.VMEM((V7X_SUBLANES, w), F32)],
        compiler_params=_params("arbitrary", "arbitrary"),
        name="rglru_seq",
    )(x, x, win, cw, cb.reshape(1, -1), wa, ba.reshape(1, -1), wi, bi.reshape(1, -1), lam.reshape(1, -1))


def _rg_state_kernel(x_ref, h0_ref, c0_ref, c1_ref, c2_ref, win_ref, cw_ref, cb_ref, wa_ref, ba_ref, wi_ref, bi_ref,
                     lam_ref, y_ref, h_ref, u_ref, a_s, b_s):
    w = RG_WIDTH
    xb = x_ref[...].astype(BF16)
    gate = jnp.dot(xb, win_ref[:, 0:w], preferred_element_type=F32)
    u = jnp.dot(xb, win_ref[:, w:2 * w], preferred_element_type=F32)
    cw = cw_ref[...]
    uc = cb_ref[...] + c0_ref[...] * cw[0:1, :] + c1_ref[...] * cw[1:2, :] + c2_ref[...] * cw[2:3, :] + u * cw[3:4, :]
    _rg_gates(uc, wa_ref, ba_ref, wi_ref, bi_ref, lam_ref, a_s, b_s)
    h = a_s[...] * h0_ref[...] + b_s[...]
    h_ref[...] = h
    u_ref[...] = u
    y_ref[...] = (jax.nn.gelu(gate, approximate=True) * h).astype(BF16)


def _rg_state(x, h0, c0, c1, c2, win, cw, cb, wa, ba, wi, bi, lam):
    m = x.shape[0]
    w = RG_WIDTH
    full = lambda shape: pl.BlockSpec(shape, lambda i: (0,) * len(shape))
    st = full((m, w))
    vec = full((1, w))
    return pl.pallas_call(
        _rg_state_kernel,
        grid=(1,),
        in_specs=[full((m, D_MODEL)), st, st, st, st, _resident(win.shape), full(cw.shape), vec,
                  _resident(wa.shape), vec, _resident(wi.shape), vec, vec],
        out_specs=[st, st, st],
        out_shape=[jax.ShapeDtypeStruct((m, w), BF16), jax.ShapeDtypeStruct((m, w), F32),
                   jax.ShapeDtypeStruct((m, w), F32)],
        scratch_shapes=[pltpu.VMEM((m, w), F32), pltpu.VMEM((m, w), F32)],
        compiler_params=_params("arbitrary"),
        name="rglru_state",
    )(x, h0, c0, c1, c2, win, cw, cb.reshape(1, -1), wa, ba.reshape(1, -1), wi, bi.reshape(1, -1), lam.reshape(1, -1))


def _page_scores(k_refs, q_b):
    return jnp.concatenate([jnp.sum(kr[0] * q_b, axis=1) for kr in k_refs], axis=1)


def _diff_decode_kernel(pt_ref, lam_ref, subln_ref, qb_ref, q_ref, kn_ref, vn_ref, *rest, n_pg, lam_init):
    del pt_ref
    k_refs, v_refs = rest[:n_pg], rest[n_pg:2 * n_pg]
    o_ref, m_s, l_s, a_s = rest[2 * n_pg:]
    step = pl.program_id(1)
    h = DIFF_HEADS

    @pl.when(step == 0)
    def _():
        s_new = jnp.sum(kn_ref[0] * q_ref[0], axis=-1, keepdims=True)
        m_s[...] = jnp.broadcast_to(s_new, m_s.shape)
        l_s[...] = jnp.full(l_s.shape, 1.0 / V7X_LANES, F32)
        a_s[...] = vn_ref[0]

    s = _page_scores(k_refs, qb_ref[0])
    m_prev = m_s[...]
    m_new = jnp.maximum(m_prev, jnp.max(s, axis=1, keepdims=True))
    alpha = jnp.exp(m_prev - m_new)
    p = jnp.exp(s - jnp.concatenate([m_new] * n_pg, axis=1))
    l_s[...] = alpha * l_s[...] + functools.reduce(
        jnp.add, [p[:, g * V7X_LANES:(g + 1) * V7X_LANES] for g in range(n_pg)])
    pb = p.astype(BF16)
    wide = PAGE_SIZE * h
    col = lax.broadcasted_iota(jnp.int32, (PAGE_SIZE, wide), 1)
    spread = jnp.where(col // h == lax.broadcasted_iota(jnp.int32, (PAGE_SIZE, wide), 0), 1.0, 0.0).astype(BF16)
    mine = (lax.broadcasted_iota(jnp.int32, (2 * h, wide), 1) % h) == (lax.broadcasted_iota(jnp.int32, (2 * h, wide), 0) // 2)
    pv = jnp.zeros(a_s.shape, F32)
    for g, vr in enumerate(v_refs):
        p_wide = jnp.dot(pb[:, g * V7X_LANES:(g + 1) * V7X_LANES], spread, preferred_element_type=F32)
        p_wide = jnp.where(mine, p_wide, 0.0).astype(BF16)
        v2 = vr[0].reshape(wide, 2 * HEAD_DIM).astype(BF16)
        pv = pv + jnp.dot(p_wide, v2, preferred_element_type=F32)
    a_s[...] = alpha * a_s[...] + pv
    m_s[...] = m_new

    @pl.when(step == pl.num_programs(1) - 1)
    def _():
        lam = _diff_lambda(lam_ref, lam_init)
        a_s[...] = a_s[...] / jnp.sum(l_s[...], axis=1, keepdims=True)
        o = a_s[pl.ds(0, h, stride=2), :] - lam * a_s[pl.ds(1, h, stride=2), :]
        o_ref[0] = _diff_finish(o, subln_ref[...], lam_init).astype(BF16)


def _decode_specs(page_table, n_pg, kshape, vshape):
    per_seq = lambda shape: pl.BlockSpec((1,) + shape, lambda b, s, pt: (b,) + (0,) * len(shape))

    def page(shape, g):
        return pl.BlockSpec((1,) + shape, lambda b, s, pt: (pt[b, s * n_pg + g],) + (0,) * len(shape))

    return per_seq, [page(kshape, g) for g in range(n_pg)] + [page(vshape, g) for g in range(n_pg)]


def _diff_decode(q, k_new, v_new, cache_k, cache_v, page_table, lam_p, subln, lam_init):
    nb, n_pages = page_table.shape
    n_pg = math.gcd(DECODE_PAGES, n_pages)
    h, c = DIFF_HEADS, 2 * DIFF_HEADS
    k_t = jnp.transpose(cache_k, (0, 2, 3, 1))
    q_b = jnp.broadcast_to(q[..., None], (nb, c, HEAD_DIM, PAGE_SIZE))
    v_rows = jnp.repeat(v_new, 2, axis=1)
    per_seq, pages = _decode_specs(page_table, n_pg, (c, HEAD_DIM, PAGE_SIZE), (PAGE_SIZE, h, 2 * HEAD_DIM))
    small = lambda shape: pl.BlockSpec(shape, lambda b, s, pt: (0,) * len(shape))
    stat = pltpu.VMEM((c, V7X_LANES), F32)
    return pl.pallas_call(
        functools.partial(_diff_decode_kernel, n_pg=n_pg, lam_init=lam_init),
        grid_spec=pltpu.PrefetchScalarGridSpec(
            num_scalar_prefetch=1,
            grid=(nb, n_pages // n_pg),
            in_specs=[small((4, HEAD_DIM)), small((1, 2 * HEAD_DIM)), per_seq((c, HEAD_DIM, PAGE_SIZE)),
                      per_seq((c, HEAD_DIM)), per_seq((c, HEAD_DIM)), per_seq((c, 2 * HEAD_DIM))] + pages,
            out_specs=per_seq((h, 2 * HEAD_DIM)),
            scratch_shapes=[stat, stat, stat],
        ),
        out_shape=jax.ShapeDtypeStruct((nb, h, 2 * HEAD_DIM), BF16),
        compiler_params=_params("arbitrary", "arbitrary"),
        name="diff_decode",
    )(page_table, lam_p, subln.reshape(1, -1), q_b, q, k_new, v_rows, *([k_t] * n_pg), *([cache_v] * n_pg))


def _moba_gate_kernel(pt_ref, qb_ref, *rest, n_pg, n_pages, n_top):
    del pt_ref
    k_refs = rest[:n_pg]
    sel_ref, s_s = rest[n_pg:]
    step = pl.program_id(1)
    c = MOBA_HEADS
    s = _page_scores(k_refs, qb_ref[0])
    for g in range(n_pg):
        s_s[step * n_pg + g] = s[:, g * V7X_LANES:(g + 1) * V7X_LANES]

    @pl.when(step == pl.num_programs(1) - 1)
    def _():
        pages_per_block = MOBA_BLOCK // PAGE_SIZE
        nblk = n_pages // pages_per_block
        gsum = jnp.sum(s_s[...], axis=2, keepdims=True).reshape(nblk, pages_per_block, c, 1)
        gate = jnp.broadcast_to(jnp.sum(gsum, axis=1), (nblk, c, V7X_LANES))
        cnt = jnp.zeros(gate.shape, F32)
        blk = lax.broadcasted_iota(jnp.int32, gate.shape, 0)
        for n in range(nblk):
            cnt = cnt + _beats(gate[n:n + 1], gate, blk > n)
        lane = lax.broadcasted_iota(jnp.int32, (c, V7X_LANES), 1)
        blk_f = blk.astype(F32)
        out = jnp.zeros((c, V7X_LANES), F32)
        for r in range(n_top):
            idx_r = jnp.sum(jnp.where(cnt == float(r), blk_f, 0.0), axis=0)
            out = jnp.where(lane == r, idx_r, out)
        sel_ref[0] = out.astype(jnp.int32)


def _moba_pick_kernel(phys_ref, qb_ref, q_ref, kn_ref, vn_ref, *rest, heads, n_chunks):
    del phys_ref
    k_refs = rest[:heads * n_chunks]
    v_refs = rest[heads * n_chunks:2 * heads * n_chunks]
    o_ref = rest[2 * heads * n_chunks]
    nt_dims = (((1,), (1,)), ((), ()))
    rows = []
    for hg in range(heads):
        q_col = qb_ref[0, 0, hg]
        ks = k_refs[hg * n_chunks:(hg + 1) * n_chunks]
        vs = v_refs[hg * n_chunks:(hg + 1) * n_chunks]
        s_new = jnp.sum(kn_ref[0, 0, hg:hg + 1, :] * q_ref[0, 0, hg:hg + 1, :], axis=-1, keepdims=True)
        s = jnp.concatenate([jnp.sum(kr[0, 0] * q_col, axis=0, keepdims=True) for kr in ks], axis=1)
        m = jnp.maximum(jnp.max(s, axis=1, keepdims=True), s_new)
        p = jnp.exp(s - m)
        w_new = jnp.exp(s_new - m)
        v_cat = jnp.concatenate([vr[0, 0] for vr in vs], axis=1).astype(BF16)
        p8 = jnp.broadcast_to(p, (V7X_SUBLANES, p.shape[1])).astype(BF16)
        pv = lax.dot_general(p8, v_cat, nt_dims, preferred_element_type=F32)[0:1, :]
        num = pv + w_new * vn_ref[0, 0, hg:hg + 1, :]
        den = jnp.sum(p, axis=1, keepdims=True) + w_new
        rows.append(num / den)
    o_ref[0, 0] = jnp.concatenate(rows, axis=0)


MOBA_PICK_HEADS = 4


def _moba_decode(q, k_new, v_new, cache_k, cache_v, page_table):
    nb, n_pages = page_table.shape
    n_pg = math.gcd(DECODE_PAGES, n_pages)
    c = MOBA_HEADS
    pages_per_block = MOBA_BLOCK // PAGE_SIZE
    assert n_pages % pages_per_block == 0
    n_top = min(MOBA_TOPK, n_pages // pages_per_block)
    k_t = jnp.transpose(cache_k, (0, 2, 3, 1))
    v_t = jnp.transpose(cache_v, (0, 2, 3, 1))
    q_b = jnp.broadcast_to(q[..., None], (nb, c, HEAD_DIM, PAGE_SIZE))
    pshape = (c, HEAD_DIM, PAGE_SIZE)
    per_seq, pages = _decode_specs(page_table, n_pg, pshape, pshape)
    sel = pl.pallas_call(
        functools.partial(_moba_gate_kernel, n_pg=n_pg, n_pages=n_pages, n_top=n_top),
        grid_spec=pltpu.PrefetchScalarGridSpec(
            num_scalar_prefetch=1,
            grid=(nb, n_pages // n_pg),
            in_specs=[per_seq(pshape)] + pages[:n_pg],
            out_specs=per_seq((c, V7X_LANES)),
            scratch_shapes=[pltpu.VMEM((n_pages, c, V7X_LANES), F32)],
        ),
        out_shape=jax.ShapeDtypeStruct((nb, c, V7X_LANES), jnp.int32),
        compiler_params=_params("arbitrary", "arbitrary"),
        name="moba_gate",
    )(page_table, q_b, *([k_t] * n_pg))
    hg = MOBA_PICK_HEADS
    n_chunks = n_top * pages_per_block
    logical = sel[:, :, :n_top, None] * pages_per_block + jnp.arange(pages_per_block, dtype=jnp.int32)
    phys = jnp.take_along_axis(page_table[:, None, :], logical.reshape(nb, c, -1), axis=2)
    grp = lambda shape: pl.BlockSpec((1, 1, hg) + shape, lambda b, g, ph: (b, g, 0) + (0,) * len(shape))

    def chunk(h, j):
        return pl.BlockSpec((1, 1, HEAD_DIM, PAGE_SIZE), lambda b, g, ph: (ph[b, g * hg + h, j], g * hg + h, 0, 0))

    chunks = [chunk(h, j) for h in range(hg) for j in range(n_chunks)]
    out = pl.pallas_call(
        functools.partial(_moba_pick_kernel, heads=hg, n_chunks=n_chunks),
        grid_spec=pltpu.PrefetchScalarGridSpec(
            num_scalar_prefetch=1,
            grid=(nb, c // hg),
            in_specs=[grp((HEAD_DIM, PAGE_SIZE)), grp((HEAD_DIM,)), grp((HEAD_DIM,)), grp((HEAD_DIM,))] + chunks + chunks,
            out_specs=pl.BlockSpec((1, 1, hg, HEAD_DIM), lambda b, g, ph: (b, g, 0, 0)),
        ),
        out_shape=jax.ShapeDtypeStruct((nb, c // hg, hg, HEAD_DIM), F32),
        compiler_params=_params("arbitrary", "arbitrary"),
        name="moba_pick",
    )(phys, q_b.reshape(nb, c // hg, hg, HEAD_DIM, PAGE_SIZE), q.reshape(nb, c // hg, hg, HEAD_DIM),
      k_new.reshape(nb, c // hg, hg, HEAD_DIM), v_new.reshape(nb, c // hg, hg, HEAD_DIM),
      *([k_t] * (hg * n_chunks)), *([v_t] * (hg * n_chunks)))
    return out.reshape(nb, c, HEAD_DIM).astype(BF16)


def kernel(x_prompt, x_sample, cache_k_l0, cache_v_l0, cache_k_l1, cache_v_l1, cache_k_l3, cache_v_l3, state_rg_h_l2, state_rg_conv_l2, state_ffn_conv, page_table, l0_w_qkv, l0_lambda, l0_subln, l0_w_o, l1_w_qkv, l1_w_o, l2_w_in, l2_conv_w, l2_conv_b, l2_w_a, l2_b_a, l2_w_i, l2_b_i, l2_lambda, l2_w_o, l3_w_qkv, l3_lambda, l3_subln, l3_w_o, ffn_w_up, ffn_conv_w, ffn_conv_b, ffn_w_down, ln_mix_g, ln_mix_b, ln_ffn_g, ln_ffn_b):
    nb, t, d = x_prompt.shape
    ns = x_sample.shape[0]
    past_len = page_table.shape[1] * PAGE_SIZE
    tm = min(ROW_TILE, t)
    tab_p = _rope_tables(jnp.arange(t, dtype=jnp.int32))
    tab_s = _rope_tables(jnp.full((ns,), past_len, dtype=jnp.int32))
    xp = x_prompt.reshape(nb * t, d)
    xs = x_sample.reshape(ns, d)
    outs_p, outs_s, ffn_p, ffn_s = {}, {}, [], []

    def attn_layer(i, w_qkv, w_o, cache_k, cache_v, lam_p=None, subln=None):
        nonlocal xp, xs
        wq = w_qkv.astype(BF16)
        wo = w_o.astype(BF16)
        diff = lam_p is not None
        res_p = _qkv_rope(xp, wq, tab_p, tm=tm, rows_per_seq=t, with_means=not diff, k_t=True, v_t=not diff)
        q_p, kb_p, vb_p, k_p, v_p = res_p[:5]
        q_s, _, _, k_s, v_s = _qkv_rope(xs, wq, tab_s, tm=ns, rows_per_seq=ns, with_means=False)
        shp = lambda a: a.reshape(nb, t, d)
        q_s3 = q_s.astype(F32).reshape(ns, 2 * DIFF_HEADS, HEAD_DIM)
        k_s3 = k_s.reshape(ns, 2 * DIFF_HEADS, HEAD_DIM)
        if diff:
            lam_init = 0.8 - 0.6 * math.exp(-0.3 * i)
            o_p = _diff_attn(shp(q_p), shp(kb_p), shp(vb_p), lam_p, subln, lam_init, tq=min(ATT_TQ, t))
            v_s3 = v_s.reshape(ns, DIFF_HEADS, 2 * HEAD_DIM)
            o_s = _diff_decode(q_s3, k_s3, v_s3, cache_k, cache_v, page_table, lam_p, subln, lam_init)
            kv_shape = ((2 * DIFF_HEADS, HEAD_DIM), (DIFF_HEADS, 2 * HEAD_DIM))
        else:
            kmeans = res_p[5].reshape(nb, t // MOBA_BLOCK, d)
            o_p = _moba_attn(shp(q_p), shp(kb_p), shp(vb_p), kmeans, tq=min(ATT_TQ, t))
            v_s3 = v_s.reshape(ns, MOBA_HEADS, HEAD_DIM)
            o_s = _moba_decode(q_s3, k_s3, v_s3, cache_k, cache_v, page_table)
            kv_shape = ((MOBA_HEADS, HEAD_DIM), (MOBA_HEADS, HEAD_DIM))
        back = lambda a: jnp.transpose(a, (0, 3, 1, 2))
        outs_p[i] = (back(k_p), v_p.reshape(nb, t, *kv_shape[1]) if diff else back(v_p))
        outs_s[i] = (k_s.reshape(ns, 1, *kv_shape[0]), v_s.reshape(ns, 1, *kv_shape[1]))
        xp = _proj_ln(o_p.reshape(nb * t, d), xp, wo, ln_mix_g[i], ln_mix_b[i], tm=tm)
        xs = _proj_ln(o_s.reshape(ns, d), xs, wo, ln_mix_g[i], ln_mix_b[i], tm=ns)

    def rg_layer(i):
        nonlocal xp, xs
        win = l2_w_in.astype(BF16)
        wa = l2_w_a.astype(BF16)
        wi = l2_w_i.astype(BF16)
        wo = l2_w_o.astype(BF16)
        y_p, hl_p, tail_p = _rg_seq(xp, win, l2_conv_w, l2_conv_b, wa, l2_b_a, wi, l2_b_i, l2_lambda, nb=nb, tm=tm)
        cs = state_rg_conv_l2
        y_s, h_s, u_s = _rg_state(xs, state_rg_h_l2, cs[:, 0], cs[:, 1], cs[:, 2], win, l2_conv_w, l2_conv_b,
                                  wa, l2_b_a, wi, l2_b_i, l2_lambda)
        outs_p[i] = (hl_p[:, V7X_SUBLANES - 1], tail_p[:, V7X_SUBLANES - (RG_CONV - 1):])
        outs_s[i] = (h_s, jnp.concatenate([cs[:, 1:], u_s[:, None]], axis=1))
        xp = _proj_ln(y_p, xp, wo, ln_mix_g[i], ln_mix_b[i], tm=tm)
        xs = _proj_ln(y_s, xs, wo, ln_mix_g[i], ln_mix_b[i], tm=ns)

    def ffn_layer(i):
        nonlocal xp, xs
        wup = ffn_w_up[i].astype(BF16)
        wdn = ffn_w_down[i].astype(BF16)
        xp, tail = _ffn_seq(xp, wup, wdn, ffn_conv_w[i], ffn_conv_b[i], ln_ffn_g[i], ln_ffn_b[i], nb=nb,
                            tm=min(FFN_ROW_TILE, t))
        st = state_ffn_conv[i]
        xs, a_s = _ffn_state(xs, st[:, 0], st[:, 1], wup, wdn, ffn_conv_w[i], ffn_conv_b[i], ln_ffn_g[i], ln_ffn_b[i])
        ffn_p.append(tail[:, V7X_SUBLANES - (FFN_CONV - 1):])
        ffn_s.append(jnp.concatenate([st[:, 1:], a_s[:, None]], axis=1))

    attn_layer(0, l0_w_qkv, l0_w_o, cache_k_l0, cache_v_l0, l0_lambda, l0_subln)
    ffn_layer(0)
    attn_layer(1, l1_w_qkv, l1_w_o, cache_k_l1, cache_v_l1)
    ffn_layer(1)
    rg_layer(2)
    ffn_layer(2)
    attn_layer(3, l3_w_qkv, l3_w_o, cache_k_l3, cache_v_l3, l3_lambda, l3_subln)
    ffn_layer(3)

    return (xp.reshape(nb, t, d), xs.reshape(ns, 1, d),
            outs_p[0][0], outs_p[0][1], outs_p[1][0], outs_p[1][1], outs_p[2][0], outs_p[2][1],
            outs_p[3][0], outs_p[3][1], jnp.stack(ffn_p),
            outs_s[0][0], outs_s[0][1], outs_s[1][0], outs_s[1][1], outs_s[2][0], outs_s[2][1],
            outs_s[3][0], outs_s[3][1], jnp.stack(ffn_s))
```

```python
import functools
import math

import jax
import jax.numpy as jnp
from jax import lax
from jax.experimental import pallas as pl
from jax.experimental.pallas import tpu as pltpu

F32 = jnp.float32
BF16 = jnp.bfloat16

D_MODEL = 1024
DEPTH = 4
PAGE_SIZE = 128
HEAD_DIM = 64
DIFF_HEADS = D_MODEL // (2 * HEAD_DIM)
MOBA_HEADS = D_MODEL // HEAD_DIM
MOBA_BLOCK = 256
MOBA_TOPK = 3
RG_WIDTH = 3 * D_MODEL // 2
RG_BLOCKS = 12
RG_BW = RG_WIDTH // RG_BLOCKS
RG_CONV = 4
RG_C = 8.0
FFN_DIM = 2816
FFN_CONV = 3
ROPE_THETA = 10000.0
NORM_EPS = 1e-5
DEEPNORM_ALPHA = (2.0 * DEPTH) ** 0.25

V7X_LANES = 128
V7X_SUBLANES = 8
V7X_VMEM_BYTES = 64 * 1024 * 1024
VMEM_LIMIT = V7X_VMEM_BYTES * 7 // 8

NEG_BIG = -1e30

ROW_TILE = 512
FFN_ROW_TILE = 1024
FFN_CHUNK = 1024
ATT_TQ = 512
DECODE_PAGES = 16


def _params(*sem):
    return pltpu.CompilerParams(dimension_semantics=sem, vmem_limit_bytes=VMEM_LIMIT)


def _resident(shape):
    nd = len(shape)
    return pl.BlockSpec(shape, lambda *_: (0,) * nd, pipeline_mode=pl.Buffered(1))


def _layer_norm(z, g, b):
    mu = jnp.mean(z, axis=-1, keepdims=True)
    zc = z - mu
    var = jnp.mean(zc * zc, axis=-1, keepdims=True)
    return zc * lax.rsqrt(var + NORM_EPS) * g + b


def _shift_rows(a, halo, s):
    row = lax.broadcasted_iota(jnp.int32, a.shape, 0)
    out = pltpu.roll(a, s, axis=0)
    hn = halo.shape[0]
    for r in range(s):
        out = jnp.where(row == r, halo[hn - s + r:hn - s + r + 1, :], out)
    return out


def _rope_cols(t, cos, sa, sb):
    return t * cos + pltpu.roll(t, HEAD_DIM // 2, axis=1) * sa + pltpu.roll(t, V7X_LANES - HEAD_DIM // 2, axis=1) * sb


def _store_heads(out_ref, cols, x, transposed):
    if transposed:
        pair = V7X_LANES // HEAD_DIM
        first = cols.start // HEAD_DIM
        out_ref[0, first:first + pair, :, :] = x.T.reshape(pair, HEAD_DIM, x.shape[0])
    else:
        out_ref[:, cols] = x


def _qkv_rope_kernel(x_ref, w_ref, cos_ref, sa_ref, sb_ref, q_ref, kb_ref, vb_ref, k_ref, v_ref, *rest,
                     q_scale, n_mean_blocks, k_t, v_t):
    xb = x_ref[...].astype(BF16)
    cos, sa, sb = cos_ref[...], sa_ref[...], sb_ref[...]
    d = D_MODEL
    q = jnp.dot(xb, w_ref[:, 0:d], preferred_element_type=F32)
    k = jnp.dot(xb, w_ref[:, d:2 * d], preferred_element_type=F32)
    v = jnp.dot(xb, w_ref[:, 2 * d:3 * d], preferred_element_type=F32)
    vb_ref[...] = v.astype(BF16)
    sums = [[] for _ in range(n_mean_blocks)]
    for c in range(d // V7X_LANES):
        sl = slice(c * V7X_LANES, (c + 1) * V7X_LANES)
        qc = _rope_cols(q[:, sl], cos, sa, sb)
        kc = _rope_cols(k[:, sl], cos, sa, sb)
        q_ref[:, sl] = (qc * q_scale).astype(BF16)
        kb_ref[:, sl] = kc.astype(BF16)
        _store_heads(k_ref, sl, kc, k_t)
        _store_heads(v_ref, sl, v[:, sl], v_t)
        for r in range(n_mean_blocks):
            sums[r].append(jnp.mean(kc[r * MOBA_BLOCK:(r + 1) * MOBA_BLOCK, :], axis=0, keepdims=True))
    for r in range(n_mean_blocks):
        rest[0][r] = jnp.concatenate(sums[r], axis=1)


def _rope_tables(pos):
    half = HEAD_DIM // 2
    inv_freq = ROPE_THETA ** (-jnp.arange(half, dtype=F32) / half)
    ang = pos.astype(F32)[:, None] * inv_freq[None, :]
    cos = jnp.cos(ang)
    sin = jnp.sin(ang)
    zero = jnp.zeros_like(sin)
    cos128 = jnp.tile(cos, (1, 4))
    sa128 = jnp.tile(jnp.concatenate([zero, sin], axis=1), (1, 2))
    sb128 = jnp.tile(jnp.concatenate([-sin, zero], axis=1), (1, 2))
    return cos128, sa128, sb128


def _qkv_rope(x, w, tables, *, tm, rows_per_seq, with_means, k_t=False, v_t=False):
    m = x.shape[0]
    nt = rows_per_seq // tm
    n_mean = tm // MOBA_BLOCK if with_means else 0
    row = pl.BlockSpec((tm, D_MODEL), lambda i: (i, 0))
    tab = pl.BlockSpec((tm, V7X_LANES), lambda i: (i % nt, 0))
    heads = D_MODEL // HEAD_DIM
    row_f32 = jax.ShapeDtypeStruct((m, D_MODEL), F32)
    t_f32 = jax.ShapeDtypeStruct((m // rows_per_seq, heads, HEAD_DIM, rows_per_seq), F32)
    t_spec = pl.BlockSpec((1, heads, HEAD_DIM, tm), lambda i: (i // nt, 0, 0, i % nt))
    bf = jax.ShapeDtypeStruct((m, D_MODEL), BF16)
    out_shape = [bf, bf, bf, t_f32 if k_t else row_f32, t_f32 if v_t else row_f32]
    out_specs = [row, row, row, t_spec if k_t else row, t_spec if v_t else row]
    if with_means:
        out_shape.append(jax.ShapeDtypeStruct((m // MOBA_BLOCK, 1, D_MODEL), F32))
        out_specs.append(pl.BlockSpec((n_mean, 1, D_MODEL), lambda i: (i, 0, 0)))
    return pl.pallas_call(
        functools.partial(_qkv_rope_kernel, q_scale=HEAD_DIM ** -0.5, n_mean_blocks=n_mean, k_t=k_t, v_t=v_t),
        grid=(m // tm,),
        in_specs=[row, _resident(w.shape), tab, tab, tab],
        out_specs=out_specs,
        out_shape=out_shape,
        compiler_params=_params("arbitrary"),
        name="qkv_rope",
    )(x, w, *tables)


def _proj_ln_kernel(o_ref, x_ref, w_ref, g_ref, b_ref, out_ref):
    y = jnp.dot(o_ref[...], w_ref[...], preferred_element_type=F32)
    out_ref[...] = _layer_norm(DEEPNORM_ALPHA * x_ref[...] + y, g_ref[...], b_ref[...])


def _proj_ln(o, x, w, g, b, *, tm):
    m, kd = o.shape
    row_o = pl.BlockSpec((tm, kd), lambda i: (i, 0))
    row_x = pl.BlockSpec((tm, D_MODEL), lambda i: (i, 0))
    vec = pl.BlockSpec((1, D_MODEL), lambda i: (0, 0))
    return pl.pallas_call(
        _proj_ln_kernel,
        grid=(m // tm,),
        in_specs=[row_o, row_x, _resident(w.shape), vec, vec],
        out_specs=row_x,
        out_shape=jax.ShapeDtypeStruct((m, D_MODEL), F32),
        compiler_params=_params("arbitrary"),
        name="proj_ln",
    )(o, x, w, g.reshape(1, -1), b.reshape(1, -1))


def _ffn_chunks(xb, xh, p_refs, wup_ref, wdn_ref, cw_ref, cb_ref, acc_ref, a_out_ref, write_tail):
    tm = xb.shape[0]

    def chunk(off, width):
        wa = wup_ref[:, off:off + width]
        wg = wup_ref[:, FFN_DIM + off:FFN_DIM + off + width]
        a = jnp.dot(xb, wa, preferred_element_type=F32)
        g = jnp.dot(xb, wg, preferred_element_type=F32)
        if xh is not None:
            ah = jnp.dot(xh, wa, preferred_element_type=F32)
            p1 = _shift_rows(a, ah, 1)
            p2 = _shift_rows(a, ah, 2)
        else:
            p2 = p_refs[0][:, off:off + width]
            p1 = p_refs[1][:, off:off + width]
        cw = cw_ref[:, off:off + width]
        cb = cb_ref[:, off:off + width]
        ac = cb + p2 * cw[0:1, :] + p1 * cw[1:2, :] + a * cw[2:3, :]
        acc_ref[:, off:off + width] = (jax.nn.gelu(ac, approximate=True) * g).astype(BF16)
        if write_tail is None:
            a_out_ref[:, off:off + width] = a
        else:
            @pl.when(write_tail)
            def _():
                a_out_ref[0, :, off:off + width] = a[tm - V7X_SUBLANES:tm, :]

    for off in range(0, FFN_DIM, FFN_CHUNK):
        chunk(off, min(FFN_CHUNK, FFN_DIM - off))
    return jnp.dot(acc_ref[...], wdn_ref[...], preferred_element_type=F32)


def _ffn_seq_kernel(x_ref, xh_ref, wup_ref, wdn_ref, cw_ref, cb_ref, g_ref, b_ref, out_ref, tail_ref, acc_ref):
    i = pl.program_id(1)
    x = x_ref[...]
    xb = x.astype(BF16)
    xh = jnp.where(i == 0, 0.0, xh_ref[...]).astype(BF16)
    y = _ffn_chunks(xb, xh, None, wup_ref, wdn_ref, cw_ref, cb_ref, acc_ref, tail_ref, i == pl.num_programs(1) - 1)
    out_ref[...] = _layer_norm(DEEPNORM_ALPHA * x + y, g_ref[...], b_ref[...])


def _ffn_state_kernel(x_ref, p2_ref, p1_ref, wup_ref, wdn_ref, cw_ref, cb_ref, g_ref, b_ref, out_ref, a_ref, acc_ref):
    x = x_ref[...]
    y = _ffn_chunks(x.astype(BF16), None, (p2_ref, p1_ref), wup_ref, wdn_ref, cw_ref, cb_ref, acc_ref, a_ref, None)
    out_ref[...] = _layer_norm(DEEPNORM_ALPHA * x + y, g_ref[...], b_ref[...])


def _ffn_seq(x, wup, wdn, cw, cb, g, b, *, nb, tm):
    m = x.shape[0]
    nt = m // nb // tm
    hb = tm // V7X_SUBLANES
    row = pl.BlockSpec((tm, D_MODEL), lambda s, i: (s * nt + i, 0))
    halo = pl.BlockSpec((V7X_SUBLANES, D_MODEL), lambda s, i: (jnp.maximum((s * nt + i) * hb - 1, 0), 0))
    vec_d = pl.BlockSpec((1, D_MODEL), lambda s, i: (0, 0))
    return pl.pallas_call(
        _ffn_seq_kernel,
        grid=(nb, nt),
        in_specs=[row, halo, _resident(wup.shape), _resident(wdn.shape), _resident(cw.shape),
                  _resident((1, FFN_DIM)), vec_d, vec_d],
        out_specs=[row, pl.BlockSpec((1, V7X_SUBLANES, FFN_DIM), lambda s, i: (s, 0, 0))],
        out_shape=[jax.ShapeDtypeStruct((m, D_MODEL), F32),
                   jax.ShapeDtypeStruct((nb, V7X_SUBLANES, FFN_DIM), F32)],
        scratch_shapes=[pltpu.VMEM((tm, FFN_DIM), BF16)],
        compiler_params=_params("arbitrary", "arbitrary"),
        name="conv_ffn_seq",
    )(x, x, wup, wdn, cw, cb.reshape(1, -1), g.reshape(1, -1), b.reshape(1, -1))


def _ffn_state(x, p2, p1, wup, wdn, cw, cb, g, b):
    m = x.shape[0]
    full = lambda shape: pl.BlockSpec(shape, lambda i: (0,) * len(shape))
    return pl.pallas_call(
        _ffn_state_kernel,
        grid=(1,),
        in_specs=[full((m, D_MODEL)), full((m, FFN_DIM)), full((m, FFN_DIM)), _resident(wup.shape),
                  _resident(wdn.shape), _resident(cw.shape), _resident((1, FFN_DIM)),
                  full((1, D_MODEL)), full((1, D_MODEL))],
        out_specs=[full((m, D_MODEL)), full((m, FFN_DIM))],
        out_shape=[jax.ShapeDtypeStruct((m, D_MODEL), F32), jax.ShapeDtypeStruct((m, FFN_DIM), F32)],
        scratch_shapes=[pltpu.VMEM((m, FFN_DIM), BF16)],
        compiler_params=_params("arbitrary"),
        name="conv_ffn_state",
    )(x, p2, p1, wup, wdn, cw, cb.reshape(1, -1), g.reshape(1, -1), b.reshape(1, -1))


def _softmax_step(s, v, m_ref, l_ref, acc_ref):
    nl = s.shape[1] // V7X_LANES
    tiles = [s[:, j * V7X_LANES:(j + 1) * V7X_LANES] for j in range(nl)]
    m_prev = m_ref[...]
    m_new = jnp.maximum(m_prev, jnp.max(functools.reduce(jnp.maximum, tiles), axis=1, keepdims=True))
    alpha = jnp.exp(m_prev - m_new)
    ps = [jnp.exp(t - m_new) for t in tiles]
    l_ref[...] = alpha * l_ref[...] + functools.reduce(jnp.add, ps)
    p = jnp.concatenate(ps, axis=1).astype(BF16)
    acc_ref[...] = alpha * acc_ref[...] + jnp.dot(p, v, preferred_element_type=F32)
    m_ref[...] = m_new


def _diff_lambda(lam_ref, lam_init):
    lp = lam_ref[...]
    s1 = jnp.sum(lp[0:1, :] * lp[1:2, :], axis=1, keepdims=True)
    s2 = jnp.sum(lp[2:3, :] * lp[3:4, :], axis=1, keepdims=True)
    return jnp.exp(s1) - jnp.exp(s2) + lam_init


def _diff_finish(o, subln, lam_init):
    ms = jnp.mean(o * o, axis=-1, keepdims=True)
    return o * lax.rsqrt(ms + NORM_EPS) * subln * (1.0 - lam_init)


def _flash_sweep(qi, tq, step, diagonal):
    lax.fori_loop(0, qi // 4, lambda j, c: (step(j * (4 * tq), 4 * tq), c)[1], 0)

    @pl.when(qi % 4 >= 2)
    def _():
        step((qi // 4) * (4 * tq), 2 * tq)

    @pl.when(qi % 2 == 1)
    def _():
        step((qi - 1) * tq, tq)

    diagonal(qi * tq)


def _stack_halves(x1, x2):
    half = x1.shape[0] // 2
    return jnp.concatenate([x1[0:half], x2[0:half], x1[half:], x2[half:]], axis=0)


def _unstack_halves(x):
    q = x.shape[0] // 4
    return (jnp.concatenate([x[0:q], x[2 * q:3 * q]], axis=0), jnp.concatenate([x[q:2 * q], x[3 * q:]], axis=0))


def _diagonal_chunk(koff, tq, scores, values, m_s, l_s, a_s):
    half = tq // 2
    tri = lax.broadcasted_iota(jnp.int32, (half, half), 1) <= lax.broadcasted_iota(jnp.int32, (half, half), 0)
    tri = jnp.where(tri, 1.0, 0.0)
    ones = jnp.ones((half, half), F32)
    keep_a = jnp.concatenate([tri, tri, ones, ones], axis=0) > 0.5
    s_a = jnp.where(keep_a, scores(slice(0, 2 * tq), koff, half), NEG_BIG)
    _softmax_step(s_a, values(koff, half), m_s, l_s, a_s)
    lo = slice(tq, 2 * tq)
    keep_b = jnp.concatenate([tri, tri], axis=0) > 0.5
    s_b = jnp.where(keep_b, scores(lo, koff + half, half), NEG_BIG)
    _softmax_step(s_b, values(koff + half, half), m_s.at[lo], l_s.at[lo], a_s.at[lo])


def _diff_attn_kernel(lam_ref, subln_ref, q_ref, k_ref, v_ref, o_ref, m_s, l_s, a_s, *, tq, lam_init):
    qi = pl.program_id(2)
    q = q_ref[0]
    lane = lax.broadcasted_iota(jnp.int32, q.shape, 1)
    zero = jnp.zeros_like(q)
    q12 = _stack_halves(jnp.where(lane < HEAD_DIM, q, zero), jnp.where(lane >= HEAD_DIM, q, zero))
    m_s[...] = jnp.full(m_s.shape, NEG_BIG, F32)
    l_s[...] = jnp.zeros(l_s.shape, F32)
    a_s[...] = jnp.zeros(a_s.shape, F32)
    nt_dims = (((1,), (1,)), ((), ()))
    half = tq // 2

    def scores(rows, koff, width):
        off = pl.multiple_of(koff, half)
        return lax.dot_general(q12[rows], k_ref[0, pl.ds(off, width), :], nt_dims, preferred_element_type=F32)

    def values(koff, width):
        return v_ref[0, pl.ds(pl.multiple_of(koff, half), width), :]

    def step(koff, width):
        _softmax_step(scores(slice(0, 2 * tq), koff, width), values(koff, width), m_s, l_s, a_s)

    _flash_sweep(qi, tq, step, lambda koff: _diagonal_chunk(koff, tq, scores, values, m_s, l_s, a_s))

    lam = _diff_lambda(lam_ref, lam_init)
    on1, on2 = _unstack_halves(a_s[...] / jnp.sum(l_s[...], axis=1, keepdims=True))
    o_ref[0] = _diff_finish(on1 - lam * on2, subln_ref[...], lam_init).astype(BF16)


def _diff_attn(q, k, v, lam_p, subln, lam_init, *, tq):
    nb, t, _ = q.shape
    qspec = pl.BlockSpec((1, tq, V7X_LANES), lambda b, h, i: (b, i, h))
    kvspec = pl.BlockSpec((1, t, V7X_LANES), lambda b, h, i: (b, 0, h))
    small = lambda shape: pl.BlockSpec(shape, lambda b, h, i: (0,) * len(shape))
    stat = pltpu.VMEM((2 * tq, V7X_LANES), F32)
    return pl.pallas_call(
        functools.partial(_diff_attn_kernel, tq=tq, lam_init=lam_init),
        grid=(nb, DIFF_HEADS, t // tq),
        in_specs=[small((4, HEAD_DIM)), small((1, 2 * HEAD_DIM)), qspec, kvspec, kvspec],
        out_specs=qspec,
        out_shape=jax.ShapeDtypeStruct((nb, t, D_MODEL), BF16),
        scratch_shapes=[stat, stat, stat],
        compiler_params=_params("arbitrary", "arbitrary", "arbitrary"),
        name="diff_attn",
    )(lam_p, subln.reshape(1, -1), q, k, v)


def _beats(row, g, tie_wins):
    return jnp.where(tie_wins, jnp.where(row >= g, 1.0, 0.0), jnp.where(row > g, 1.0, 0.0))


def _moba_select_bias(g_t, own, n_blocks):
    blk = lax.broadcasted_iota(jnp.int32, g_t.shape, 0)
    cnt = jnp.zeros(g_t.shape, F32)
    for n in range(n_blocks):
        cnt = cnt + _beats(g_t[n:n + 1, :], g_t, blk > n) * jnp.where(n < own, 1.0, 0.0)
    visible = jnp.where(cnt < MOBA_TOPK, 1.0, 0.0) * jnp.where(blk < own, 1.0, 0.0) + jnp.where(blk == own, 1.0, 0.0)
    return jnp.where(visible > 0.5, 0.0, NEG_BIG)


def _moba_attn_kernel(q_ref, k_ref, v_ref, mean_ref, o_ref, m_s, l_s, a_s, eye_s, *, tq, n_blocks):
    qi = pl.program_id(2)
    q = q_ref[0]
    lane = lax.broadcasted_iota(jnp.int32, q.shape, 1)
    zero = jnp.zeros_like(q)
    nt_dims = (((1,), (1,)), ((), ()))
    means = mean_ref[0].astype(BF16)
    own = (qi * tq + lax.broadcasted_iota(jnp.int32, (1, tq), 1)) // MOBA_BLOCK

    @pl.when(qi == 0)
    def _():
        same = lax.broadcasted_iota(jnp.int32, (tq, tq), 0) == lax.broadcasted_iota(jnp.int32, (tq, tq), 1)
        eye_s[...] = jnp.where(same, 1.0, 0.0).astype(BF16)

    eye = eye_s[...]
    pad = jnp.zeros((V7X_LANES - n_blocks, tq), BF16)
    qs, sels = [], []
    for qq in (jnp.where(lane < HEAD_DIM, q, zero), jnp.where(lane >= HEAD_DIM, q, zero)):
        g_t = lax.dot_general(means, qq, nt_dims, preferred_element_type=F32)
        bias_t = jnp.concatenate([_moba_select_bias(g_t, own, n_blocks).astype(BF16), pad], axis=0)
        sels.append(lax.dot_general(eye, bias_t, nt_dims, preferred_element_type=F32).astype(BF16))
        qs.append(qq)
    q_aug = jnp.concatenate([_stack_halves(*qs), _stack_halves(*sels)], axis=1)
    m_s[...] = jnp.full(m_s.shape, NEG_BIG, F32)
    l_s[...] = jnp.zeros(l_s.shape, F32)
    a_s[...] = jnp.zeros(a_s.shape, F32)
    half = tq // 2

    def scores(rows, koff, width):
        off = pl.multiple_of(koff, half)
        rel = (lax.broadcasted_iota(jnp.int32, (width, V7X_LANES), 1)
               - lax.broadcasted_iota(jnp.int32, (width, V7X_LANES), 0) // MOBA_BLOCK)
        onehot = jnp.where(rel == koff // MOBA_BLOCK, 1.0, 0.0).astype(BF16)
        k_aug = jnp.concatenate([k_ref[0, pl.ds(off, width), :], onehot], axis=1)
        return lax.dot_general(q_aug[rows], k_aug, nt_dims, preferred_element_type=F32)

    def values(koff, width):
        return v_ref[0, pl.ds(pl.multiple_of(koff, half), width), :]

    def step(koff, width):
        _softmax_step(scores(slice(0, 2 * tq), koff, width), values(koff, width), m_s, l_s, a_s)

    _flash_sweep(qi, tq, step, lambda koff: _diagonal_chunk(koff, tq, scores, values, m_s, l_s, a_s))
    on1, on2 = _unstack_halves(a_s[...] / jnp.sum(l_s[...], axis=1, keepdims=True))
    o_ref[0] = jnp.where(lane < HEAD_DIM, on1, on2).astype(BF16)


def _moba_attn(q, k, v, kmeans, *, tq):
    nb, t, _ = q.shape
    n_blocks = t // MOBA_BLOCK
    qspec = pl.BlockSpec((1, tq, V7X_LANES), lambda b, h, i: (b, i, h))
    kvspec = pl.BlockSpec((1, t, V7X_LANES), lambda b, h, i: (b, 0, h))
    mspec = pl.BlockSpec((1, n_blocks, V7X_LANES), lambda b, h, i: (b, 0, h))
    stat = pltpu.VMEM((2 * tq, V7X_LANES), F32)
    return pl.pallas_call(
        functools.partial(_moba_attn_kernel, tq=tq, n_blocks=n_blocks),
        grid=(nb, MOBA_HEADS // 2, t // tq),
        in_specs=[qspec, kvspec, kvspec, mspec],
        out_specs=qspec,
        out_shape=jax.ShapeDtypeStruct((nb, t, D_MODEL), BF16),
        scratch_shapes=[stat, stat, stat, pltpu.VMEM((tq, tq), BF16)],
        compiler_params=_params("arbitrary", "arbitrary", "arbitrary"),
        name="moba_attn",
    )(q, k, v, kmeans)


def _softplus(z):
    return jnp.maximum(z, 0.0) + jnp.log1p(jnp.exp(-jnp.abs(z)))


def _rg_gates(uc, wa_ref, ba_ref, wi_ref, bi_ref, lam_ref, a_ref, b_ref):
    sp = _softplus(-lam_ref[...])
    for n in range(RG_BLOCKS):
        sl = slice(n * RG_BW, (n + 1) * RG_BW)
        ub = uc[:, sl]
        ubb = ub.astype(BF16)
        r = jax.nn.sigmoid(jnp.dot(ubb, wa_ref[n], preferred_element_type=F32) + ba_ref[:, sl])
        ig = jax.nn.sigmoid(jnp.dot(ubb, wi_ref[n], preferred_element_type=F32) + bi_ref[:, sl])
        log_a = -RG_C * r * sp[:, sl]
        a = jnp.exp(log_a)
        a_ref[:, sl] = a
        b_ref[:, sl] = jnp.sqrt(1.0 - a * a) * (ig * ub)


def _rg_seq_kernel(x_ref, xh_ref, win_ref, cw_ref, cb_ref, wa_ref, ba_ref, wi_ref, bi_ref, lam_ref,
                   y_ref, hl_ref, tail_ref, gate_s, a_s, b_s, h_s, carry_s):
    i = pl.program_id(1)
    tm = x_ref.shape[0]
    w = RG_WIDTH
    xb = x_ref[...].astype(BF16)
    xh = jnp.where(i == 0, 0.0, xh_ref[...]).astype(BF16)
    gate_s[...] = jnp.dot(xb, win_ref[:, 0:w], preferred_element_type=F32)
    u = jnp.dot(xb, win_ref[:, w:2 * w], preferred_element_type=F32)
    uh = jnp.dot(xh, win_ref[:, w:2 * w], preferred_element_type=F32)
    cw = cw_ref[...]
    uc = cb_ref[...] + u * cw[3:4, :]
    for s in range(1, RG_CONV):
        uc = uc + _shift_rows(u, uh, s) * cw[3 - s:4 - s, :]
    _rg_gates(uc, wa_ref, ba_ref, wi_ref, bi_ref, lam_ref, a_s, b_s)

    @pl.when(i == 0)
    def _():
        carry_s[...] = jnp.zeros_like(carry_s)

    row = lax.broadcasted_iota(jnp.int32, (V7X_SUBLANES, w), 0)

    def group(gi, c):
        off = pl.multiple_of(gi * V7X_SUBLANES, V7X_SUBLANES)
        a8 = a_s[pl.ds(off, V7X_SUBLANES), :]
        b8 = b_s[pl.ds(off, V7X_SUBLANES), :]
        for s in (1, 2, 4):
            ok = row >= s
            b8 = jnp.where(ok, a8 * pltpu.roll(b8, s, axis=0) + b8, b8)
            a8 = jnp.where(ok, a8 * pltpu.roll(a8, s, axis=0), a8)
        h8 = a8 * carry_s[V7X_SUBLANES - 1:V7X_SUBLANES, :] + b8
        h_s[pl.ds(off, V7X_SUBLANES), :] = h8
        carry_s[...] = h8
        return c

    lax.fori_loop(0, tm // V7X_SUBLANES, group, 0)
    y_ref[...] = (jax.nn.gelu(gate_s[...], approximate=True) * h_s[...]).astype(BF16)

    @pl.when(i == pl.num_programs(1) - 1)
    def _():
        hl_ref[0] = carry_s[...]
        tail_ref[0] = u[tm - V7X_SUBLANES:tm, :]


def _rg_seq(x, win, cw, cb, wa, ba, wi, bi, lam, *, nb, tm):
    m = x.shape[0]
    nt = m // nb // tm
    hb = tm // V7X_SUBLANES
    w = RG_WIDTH
    row = pl.BlockSpec((tm, D_MODEL), lambda s, i: (s * nt + i, 0))
    halo = pl.BlockSpec((V7X_SUBLANES, D_MODEL), lambda s, i: (jnp.maximum((s * nt + i) * hb - 1, 0), 0))
    vec = _resident((1, w))
    per_seq = pl.BlockSpec((1, V7X_SUBLANES, w), lambda s, i: (s, 0, 0))
    big = pltpu.VMEM((tm, w), F32)
    return pl.pallas_call(
        _rg_seq_kernel,
        grid=(nb, nt),
        in_specs=[row, halo, _resident(win.shape), _resident(cw.shape), vec, _resident(wa.shape), vec,
                  _resident(wi.shape), vec, vec],
        out_specs=[pl.BlockSpec((tm, w), lambda s, i: (s * nt + i, 0)), per_seq, per_seq],
        out_shape=[jax.ShapeDtypeStruct((m, w), BF16), jax.ShapeDtypeStruct((nb, V7X_SUBLANES, w), F32),
                   jax.ShapeDtypeStruct((nb, V7X_SUBLANES, w), F32)],
        scratch_shapes=[big, big, big, big, pltpu.VMEM((V7X_SUBLANES, w), F32)],
        compiler_params=_params("arbitrary", "arbitrary"),
        name="rglru_seq",
    )(x, x, win, cw, cb.reshape(1, -1), wa, ba.reshape(1, -1), wi, bi.reshape(1, -1), lam.reshape(1, -1))


def _rg_state_kernel(x_ref, h0_ref, c0_ref, c1_ref, c2_ref, win_ref, cw_ref, cb_ref, wa_ref, ba_ref, wi_ref, bi_ref,
                     lam_ref, y_ref, h_ref, u_ref, a_s, b_s):
    w = RG_WIDTH
    xb = x_ref[...].astype(BF16)
    gate = jnp.dot(xb, win_ref[:, 0:w], preferred_element_type=F32)
    u = jnp.dot(xb, win_ref[:, w:2 * w], preferred_element_type=F32)
    cw = cw_ref[...]
    uc = cb_ref[...] + c0_ref[...] * cw[0:1, :] + c1_ref[...] * cw[1:2, :] + c2_ref[...] * cw[2:3, :] + u * cw[3:4, :]
    _rg_gates(uc, wa_ref, ba_ref, wi_ref, bi_ref, lam_ref, a_s, b_s)
    h = a_s[...] * h0_ref[...] + b_s[...]
    h_ref[...] = h
    u_ref[...] = u
    y_ref[...] = (jax.nn.gelu(gate, approximate=True) * h).astype(BF16)


def _rg_state(x, h0, c0, c1, c2, win, cw, cb, wa, ba, wi, bi, lam):
    m = x.shape[0]
    w = RG_WIDTH
    full = lambda shape: pl.BlockSpec(shape, lambda i: (0,) * len(shape))
    st = full((m, w))
    vec = full((1, w))
    return pl.pallas_call(
        _rg_state_kernel,
        grid=(1,),
        in_specs=[full((m, D_MODEL)), st, st, st, st, _resident(win.shape), full(cw.shape), vec,
                  _resident(wa.shape), vec, _resident(wi.shape), vec, vec],
        out_specs=[st, st, st],
        out_shape=[jax.ShapeDtypeStruct((m, w), BF16), jax.ShapeDtypeStruct((m, w), F32),
                   jax.ShapeDtypeStruct((m, w), F32)],
        scratch_shapes=[pltpu.VMEM((m, w), F32), pltpu.VMEM((m, w), F32)],
        compiler_params=_params("arbitrary"),
        name="rglru_state",
    )(x, h0, c0, c1, c2, win, cw, cb.reshape(1, -1), wa, ba.reshape(1, -1), wi, bi.reshape(1, -1), lam.reshape(1, -1))


def _page_scores(k_refs, q_b):
    return jnp.concatenate([jnp.sum(kr[0] * q_b, axis=1) for kr in k_refs], axis=1)


def _diff_decode_kernel(pt_ref, lam_ref, subln_ref, qb_ref, q_ref, kn_ref, vn_ref, *rest, n_pg, lam_init):
    del pt_ref
    k_refs, v_refs = rest[:n_pg], rest[n_pg:2 * n_pg]
    o_ref, m_s, l_s, a_s = rest[2 * n_pg:]
    step = pl.program_id(1)
    h = DIFF_HEADS

    @pl.when(step == 0)
    def _():
        s_new = jnp.sum(kn_ref[0] * q_ref[0], axis=-1, keepdims=True)
        m_s[...] = jnp.broadcast_to(s_new, m_s.shape)
        l_s[...] = jnp.full(l_s.shape, 1.0 / V7X_LANES, F32)
        a_s[...] = vn_ref[0]

    s = _page_scores(k_refs, qb_ref[0])
    m_prev = m_s[...]
    m_new = jnp.maximum(m_prev, jnp.max(s, axis=1, keepdims=True))
    alpha = jnp.exp(m_prev - m_new)
    p = jnp.exp(s - jnp.concatenate([m_new] * n_pg, axis=1))
    l_s[...] = alpha * l_s[...] + functools.reduce(
        jnp.add, [p[:, g * V7X_LANES:(g + 1) * V7X_LANES] for g in range(n_pg)])
    pb = p.astype(BF16)
    wide = PAGE_SIZE * h
    col = lax.broadcasted_iota(jnp.int32, (PAGE_SIZE, wide), 1)
    spread = jnp.where(col // h == lax.broadcasted_iota(jnp.int32, (PAGE_SIZE, wide), 0), 1.0, 0.0).astype(BF16)
    mine = (lax.broadcasted_iota(jnp.int32, (2 * h, wide), 1) % h) == (lax.broadcasted_iota(jnp.int32, (2 * h, wide), 0) // 2)
    pv = jnp.zeros(a_s.shape, F32)
    for g, vr in enumerate(v_refs):
        p_wide = jnp.dot(pb[:, g * V7X_LANES:(g + 1) * V7X_LANES], spread, preferred_element_type=F32)
        p_wide = jnp.where(mine, p_wide, 0.0).astype(BF16)
        v2 = vr[0].reshape(wide, 2 * HEAD_DIM).astype(BF16)
        pv = pv + jnp.dot(p_wide, v2, preferred_element_type=F32)
    a_s[...] = alpha * a_s[...] + pv
    m_s[...] = m_new

    @pl.when(step == pl.num_programs(1) - 1)
    def _():
        lam = _diff_lambda(lam_ref, lam_init)
        a_s[...] = a_s[...] / jnp.sum(l_s[...], axis=1, keepdims=True)
        o = a_s[pl.ds(0, h, stride=2), :] - lam * a_s[pl.ds(1, h, stride=2), :]
        o_ref[0] = _diff_finish(o, subln_ref[...], lam_init).astype(BF16)


def _decode_specs(page_table, n_pg, kshape, vshape):
    per_seq = lambda shape: pl.BlockSpec((1,) + shape, lambda b, s, pt: (b,) + (0,) * len(shape))

    def page(shape, g):
        return pl.BlockSpec((1,) + shape, lambda b, s, pt: (pt[b, s * n_pg + g],) + (0,) * len(shape))

    return per_seq, [page(kshape, g) for g in range(n_pg)] + [page(vshape, g) for g in range(n_pg)]


def _diff_decode(q, k_new, v_new, cache_k, cache_v, page_table, lam_p, subln, lam_init):
    nb, n_pages = page_table.shape
    n_pg = math.gcd(DECODE_PAGES, n_pages)
    h, c = DIFF_HEADS, 2 * DIFF_HEADS
    k_t = jnp.transpose(cache_k, (0, 2, 3, 1))
    q_b = jnp.broadcast_to(q[..., None], (nb, c, HEAD_DIM, PAGE_SIZE))
    v_rows = jnp.repeat(v_new, 2, axis=1)
    per_seq, pages = _decode_specs(page_table, n_pg, (c, HEAD_DIM, PAGE_SIZE), (PAGE_SIZE, h, 2 * HEAD_DIM))
    small = lambda shape: pl.BlockSpec(shape, lambda b, s, pt: (0,) * len(shape))
    stat = pltpu.VMEM((c, V7X_LANES), F32)
    return pl.pallas_call(
        functools.partial(_diff_decode_kernel, n_pg=n_pg, lam_init=lam_init),
        grid_spec=pltpu.PrefetchScalarGridSpec(
            num_scalar_prefetch=1,
            grid=(nb, n_pages // n_pg),
            in_specs=[small((4, HEAD_DIM)), small((1, 2 * HEAD_DIM)), per_seq((c, HEAD_DIM, PAGE_SIZE)),
                      per_seq((c, HEAD_DIM)), per_seq((c, HEAD_DIM)), per_seq((c, 2 * HEAD_DIM))] + pages,
            out_specs=per_seq((h, 2 * HEAD_DIM)),
            scratch_shapes=[stat, stat, stat],
        ),
        out_shape=jax.ShapeDtypeStruct((nb, h, 2 * HEAD_DIM), BF16),
        compiler_params=_params("arbitrary", "arbitrary"),
        name="diff_decode",
    )(page_table, lam_p, subln.reshape(1, -1), q_b, q, k_new, v_rows, *([k_t] * n_pg), *([cache_v] * n_pg))


def _moba_gate_kernel(pt_ref, qb_ref, *rest, n_pg, n_pages, n_top):
    del pt_ref
    k_refs = rest[:n_pg]
    sel_ref, s_s = rest[n_pg:]
    step = pl.program_id(1)
    c = MOBA_HEADS
    s = _page_scores(k_refs, qb_ref[0])
    for g in range(n_pg):
        s_s[step * n_pg + g] = s[:, g * V7X_LANES:(g + 1) * V7X_LANES]

    @pl.when(step == pl.num_programs(1) - 1)
    def _():
        pages_per_block = MOBA_BLOCK // PAGE_SIZE
        nblk = n_pages // pages_per_block
        gsum = jnp.sum(s_s[...], axis=2, keepdims=True).reshape(nblk, pages_per_block, c, 1)
        gate = jnp.broadcast_to(jnp.sum(gsum, axis=1), (nblk, c, V7X_LANES))
        cnt = jnp.zeros(gate.shape, F32)
        blk = lax.broadcasted_iota(jnp.int32, gate.shape, 0)
        for n in range(nblk):
            cnt = cnt + _beats(gate[n:n + 1], gate, blk > n)
        lane = lax.broadcasted_iota(jnp.int32, (c, V7X_LANES), 1)
        blk_f = blk.astype(F32)
        out = jnp.zeros((c, V7X_LANES), F32)
        for r in range(n_top):
            idx_r = jnp.sum(jnp.where(cnt == float(r), blk_f, 0.0), axis=0)
            out = jnp.where(lane == r, idx_r, out)
        sel_ref[0] = out.astype(jnp.int32)


def _moba_pick_kernel(phys_ref, qb_ref, q_ref, kn_ref, vn_ref, *rest, heads, n_chunks):
    del phys_ref
    k_refs = rest[:heads * n_chunks]
    v_refs = rest[heads * n_chunks:2 * heads * n_chunks]
    o_ref = rest[2 * heads * n_chunks]
    nt_dims = (((1,), (1,)), ((), ()))
    rows = []
    for hg in range(heads):
        q_col = qb_ref[0, 0, hg]
        ks = k_refs[hg * n_chunks:(hg + 1) * n_chunks]
        vs = v_refs[hg * n_chunks:(hg + 1) * n_chunks]
        s_new = jnp.sum(kn_ref[0, 0, hg:hg + 1, :] * q_ref[0, 0, hg:hg + 1, :], axis=-1, keepdims=True)
        s = jnp.concatenate([jnp.sum(kr[0, 0] * q_col, axis=0, keepdims=True) for kr in ks], axis=1)
        m = jnp.maximum(jnp.max(s, axis=1, keepdims=True), s_new)
        p = jnp.exp(s - m)
        w_new = jnp.exp(s_new - m)
        v_cat = jnp.concatenate([vr[0, 0] for vr in vs], axis=1).astype(BF16)
        p8 = jnp.broadcast_to(p, (V7X_SUBLANES, p.shape[1])).astype(BF16)
        pv = lax.dot_general(p8, v_cat, nt_dims, preferred_element_type=F32)[0:1, :]
        num = pv + w_new * vn_ref[0, 0, hg:hg + 1, :]
        den = jnp.sum(p, axis=1, keepdims=True) + w_new
        rows.append(num / den)
    o_ref[0, 0] = jnp.concatenate(rows, axis=0)


MOBA_PICK_HEADS = 4


def _moba_decode(q, k_new, v_new, cache_k, cache_v, page_table):
    nb, n_pages = page_table.shape
    n_pg = math.gcd(DECODE_PAGES, n_pages)
    c = MOBA_HEADS
    pages_per_block = MOBA_BLOCK // PAGE_SIZE
    assert n_pages % pages_per_block == 0
    n_top = min(MOBA_TOPK, n_pages // pages_per_block)
    k_t = jnp.transpose(cache_k, (0, 2, 3, 1))
    v_t = jnp.transpose(cache_v, (0, 2, 3, 1))
    q_b = jnp.broadcast_to(q[..., None], (nb, c, HEAD_DIM, PAGE_SIZE))
    pshape = (c, HEAD_DIM, PAGE_SIZE)
    per_seq, pages = _decode_specs(page_table, n_pg, pshape, pshape)
    sel = pl.pallas_call(
        functools.partial(_moba_gate_kernel, n_pg=n_pg, n_pages=n_pages, n_top=n_top),
        grid_spec=pltpu.PrefetchScalarGridSpec(
            num_scalar_prefetch=1,
            grid=(nb, n_pages // n_pg),
            in_specs=[per_seq(pshape)] + pages[:n_pg],
            out_specs=per_seq((c, V7X_LANES)),
            scratch_shapes=[pltpu.VMEM((n_pages, c, V7X_LANES), F32)],
        ),
        out_shape=jax.ShapeDtypeStruct((nb, c, V7X_LANES), jnp.int32),
        compiler_params=_params("arbitrary", "arbitrary"),
        name="moba_gate",
    )(page_table, q_b, *([k_t] * n_pg))
    hg = MOBA_PICK_HEADS
    n_chunks = n_top * pages_per_block
    logical = sel[:, :, :n_top, None] * pages_per_block + jnp.arange(pages_per_block, dtype=jnp.int32)
    phys = jnp.take_along_axis(page_table[:, None, :], logical.reshape(nb, c, -1), axis=2)
    grp = lambda shape: pl.BlockSpec((1, 1, hg) + shape, lambda b, g, ph: (b, g, 0) + (0,) * len(shape))

    def chunk(h, j):
        return pl.BlockSpec((1, 1, HEAD_DIM, PAGE_SIZE), lambda b, g, ph: (ph[b, g * hg + h, j], g * hg + h, 0, 0))

    chunks = [chunk(h, j) for h in range(hg) for j in range(n_chunks)]
    out = pl.pallas_call(
        functools.partial(_moba_pick_kernel, heads=hg, n_chunks=n_chunks),
        grid_spec=pltpu.PrefetchScalarGridSpec(
            num_scalar_prefetch=1,
            grid=(nb, c // hg),
            in_specs=[grp((HEAD_DIM, PAGE_SIZE)), grp((HEAD_DIM,)), grp((HEAD_DIM,)), grp((HEAD_DIM,))] + chunks + chunks,
            out_specs=pl.BlockSpec((1, 1, hg, HEAD_DIM), lambda b, g, ph: (b, g, 0, 0)),
        ),
        out_shape=jax.ShapeDtypeStruct((nb, c // hg, hg, HEAD_DIM), F32),
        compiler_params=_params("arbitrary", "arbitrary"),
        name="moba_pick",
    )(phys, q_b.reshape(nb, c // hg, hg, HEAD_DIM, PAGE_SIZE), q.reshape(nb, c // hg, hg, HEAD_DIM),
      k_new.reshape(nb, c // hg, hg, HEAD_DIM), v_new.reshape(nb, c // hg, hg, HEAD_DIM),
      *([k_t] * (hg * n_chunks)), *([v_t] * (hg * n_chunks)))
    return out.reshape(nb, c, HEAD_DIM).astype(BF16)


def kernel(x_prompt, x_sample, cache_k_l0, cache_v_l0, cache_k_l1, cache_v_l1, cache_k_l3, cache_v_l3, state_rg_h_l2, state_rg_conv_l2, state_ffn_conv, page_table, l0_w_qkv, l0_lambda, l0_subln, l0_w_o, l1_w_qkv, l1_w_o, l2_w_in, l2_conv_w, l2_conv_b, l2_w_a, l2_b_a, l2_w_i, l2_b_i, l2_lambda, l2_w_o, l3_w_qkv, l3_lambda, l3_subln, l3_w_o, ffn_w_up, ffn_conv_w, ffn_conv_b, ffn_w_down, ln_mix_g, ln_mix_b, ln_ffn_g, ln_ffn_b):
    nb, t, d = x_prompt.shape
    ns = x_sample.shape[0]
    past_len = page_table.shape[1] * PAGE_SIZE
    tm = min(ROW_TILE, t)
    tab_p = _rope_tables(jnp.arange(t, dtype=jnp.int32))
    tab_s = _rope_tables(jnp.full((ns,), past_len, dtype=jnp.int32))
    xp = x_prompt.reshape(nb * t, d)
    xs = x_sample.reshape(ns, d)
    outs_p, outs_s, ffn_p, ffn_s = {}, {}, [], []

    def attn_layer(i, w_qkv, w_o, cache_k, cache_v, lam_p=None, subln=None):
        nonlocal xp, xs
        wq = w_qkv.astype(BF16)
        wo = w_o.astype(BF16)
        diff = lam_p is not None
        res_p = _qkv_rope(xp, wq, tab_p, tm=tm, rows_per_seq=t, with_means=not diff, k_t=True, v_t=not diff)
        q_p, kb_p, vb_p, k_p, v_p = res_p[:5]
        q_s, _, _, k_s, v_s = _qkv_rope(xs, wq, tab_s, tm=ns, rows_per_seq=ns, with_means=False)
        shp = lambda a: a.reshape(nb, t, d)
        q_s3 = q_s.astype(F32).reshape(ns, 2 * DIFF_HEADS, HEAD_DIM)
        k_s3 = k_s.reshape(ns, 2 * DIFF_HEADS, HEAD_DIM)
        if diff:
            lam_init = 0.8 - 0.6 * math.exp(-0.3 * i)
            o_p = _diff_attn(shp(q_p), shp(kb_p), shp(vb_p), lam_p, subln, lam_init, tq=min(ATT_TQ, t))
            v_s3 = v_s.reshape(ns, DIFF_HEADS, 2 * HEAD_DIM)
            o_s = _diff_decode(q_s3, k_s3, v_s3, cache_k, cache_v, page_table, lam_p, subln, lam_init)
            kv_shape = ((2 * DIFF_HEADS, HEAD_DIM), (DIFF_HEADS, 2 * HEAD_DIM))
        else:
            kmeans = res_p[5].reshape(nb, t // MOBA_BLOCK, d)
            o_p = _moba_attn(shp(q_p), shp(kb_p), shp(vb_p), kmeans, tq=min(ATT_TQ, t))
            v_s3 = v_s.reshape(ns, MOBA_HEADS, HEAD_DIM)
            o_s = _moba_decode(q_s3, k_s3, v_s3, cache_k, cache_v, page_table)
            kv_shape = ((MOBA_HEADS, HEAD_DIM), (MOBA_HEADS, HEAD_DIM))
        back = lambda a: jnp.transpose(a, (0, 3, 1, 2))
        outs_p[i] = (back(k_p), v_p.reshape(nb, t, *kv_shape[1]) if diff else back(v_p))
        outs_s[i] = (k_s.reshape(ns, 1, *kv_shape[0]), v_s.reshape(ns, 1, *kv_shape[1]))
        xp = _proj_ln(o_p.reshape(nb * t, d), xp, wo, ln_mix_g[i], ln_mix_b[i], tm=tm)
        xs = _proj_ln(o_s.reshape(ns, d), xs, wo, ln_mix_g[i], ln_mix_b[i], tm=ns)

    def rg_layer(i):
        nonlocal xp, xs
        win = l2_w_in.astype(BF16)
        wa = l2_w_a.astype(BF16)
        wi = l2_w_i.astype(BF16)
        wo = l2_w_o.astype(BF16)
        y_p, hl_p, tail_p = _rg_seq(xp, win, l2_conv_w, l2_conv_b, wa, l2_b_a, wi, l2_b_i, l2_lambda, nb=nb, tm=tm)
        cs = state_rg_conv_l2
        y_s, h_s, u_s = _rg_state(xs, state_rg_h_l2, cs[:, 0], cs[:, 1], cs[:, 2], win, l2_conv_w, l2_conv_b,
                                  wa, l2_b_a, wi, l2_b_i, l2_lambda)
        outs_p[i] = (hl_p[:, V7X_SUBLANES - 1], tail_p[:, V7X_SUBLANES - (RG_CONV - 1):])
        outs_s[i] = (h_s, jnp.concatenate([cs[:, 1:], u_s[:, None]], axis=1))
        xp = _proj_ln(y_p, xp, wo, ln_mix_g[i], ln_mix_b[i], tm=tm)
        xs = _proj_ln(y_s, xs, wo, ln_mix_g[i], ln_mix_b[i], tm=ns)

    def ffn_layer(i):
        nonlocal xp, xs
        wup = ffn_w_up[i].astype(BF16)
        wdn = ffn_w_down[i].astype(BF16)
        xp, tail = _ffn_seq(xp, wup, wdn, ffn_conv_w[i], ffn_conv_b[i], ln_ffn_g[i], ln_ffn_b[i], nb=nb,
                            tm=min(FFN_ROW_TILE, t))
        st = state_ffn_conv[i]
        xs, a_s = _ffn_state(xs, st[:, 0], st[:, 1], wup, wdn, ffn_conv_w[i], ffn_conv_b[i], ln_ffn_g[i], ln_ffn_b[i])
        ffn_p.append(tail[:, V7X_SUBLANES - (FFN_CONV - 1):])
        ffn_s.append(jnp.concatenate([st[:, 1:], a_s[:, None]], axis=1))

    attn_layer(0, l0_w_qkv, l0_w_o, cache_k_l0, cache_v_l0, l0_lambda, l0_subln)
    ffn_layer(0)
    attn_layer(1, l1_w_qkv, l1_w_o, cache_k_l1, cache_v_l1)
    ffn_layer(1)
    rg_layer(2)
    ffn_layer(2)
    attn_layer(3, l3_w_qkv, l3_w_o, cache_k_l3, cache_v_l3, l3_lambda, l3_subln)
    ffn_layer(3)

    return (xp.reshape(nb, t, d), xs.reshape(ns, 1, d),
            outs_p[0][0], outs_p[0][1], outs_p[1][0], outs_p[1][1], outs_p[2][0], outs_p[2][1],
            outs_p[3][0], outs_p[3][1], jnp.stack(ffn_p),
            outs_s[0][0], outs_s[0][1], outs_s[1][0], outs_s[1][1], outs_s[2][0], outs_s[2][1],
            outs_s[3][0], outs_s[3][1], jnp.stack(ffn_s))
```
